```python
import math
import jax, jax.numpy as jnp
from jax import lax
import numpy as np

D_MODEL = 2048
BATCH = 4
SEQ = 2048
DEPTH = 4
DEC_BATCH = 8
DEC_SEQ = 4
PAST_LEN = 16384
PAGE_SIZE = 128

N_MIXERS = 3
N_SSD_LAYERS = (DEPTH + 2) // 3
N_MOBA_LAYERS = (DEPTH + 1) // 3
N_GMLP_LAYERS = DEPTH // 3
NORM_EPS = 1e-6

SSD_EXPAND = 2
SSD_INNER = SSD_EXPAND * D_MODEL
SSD_HEAD_DIM = 64
SSD_HEADS = SSD_INNER // SSD_HEAD_DIM
SSD_GROUPS = 8
SSD_HPG = SSD_HEADS // SSD_GROUPS
SSD_STATE = 128
SSD_CONV = 4
SSD_CHUNK = 128
SSD_CONV_DIM = SSD_INNER + 2 * SSD_GROUPS * SSD_STATE
SSD_IN_DIM = SSD_INNER + SSD_CONV_DIM + SSD_HEADS

MOBA_HEADS = 16
MOBA_HEAD_DIM = D_MODEL // MOBA_HEADS
MOBA_BLOCK = 256
MOBA_TOPK = 3
MOBA_Q_BLOCK = 8
ROPE_THETA = 500000.0
ROPE_DIM = MOBA_HEAD_DIM // 4

GMLP_DIM = D_MODEL
GMLP_GROUPS = 16
GMLP_GROUP_DIM = GMLP_DIM // GMLP_GROUPS
GMLP_CHUNK = 128

FFN_HIDDEN = ((8 * D_MODEL + 3 * 256 - 1) // (3 * 256)) * 256

kernel_name = 'hybrid_ssd_moba_gmlp_decode_step'


def rms_norm(x, g):
    xf = x.astype(jnp.float32)
    y = xf * lax.rsqrt(jnp.mean(xf * xf, axis=-1, keepdims=True) + NORM_EPS)
    return (y * g.astype(jnp.float32)).astype(x.dtype)


def swiglu_ffn(h, w_gate, w_up, w_down):
    return (jax.nn.silu(h @ w_gate) * (h @ w_up)) @ w_down


def partial_rotary(x, pos):
    half = ROPE_DIM // 2
    inv_freq = ROPE_THETA ** (-jnp.arange(half, dtype=jnp.float32) * 2.0 / ROPE_DIM)
    ang = pos.astype(jnp.float32)[:, None] * inv_freq[None, :]
    cos = jnp.cos(ang)[None, :, None, :]
    sin = jnp.sin(ang)[None, :, None, :]
    xf = x.astype(jnp.float32)
    x1 = xf[..., :half]
    x2 = xf[..., half:ROPE_DIM]
    out = jnp.concatenate([x1 * cos - x2 * sin, x1 * sin + x2 * cos, xf[..., ROPE_DIM:]], axis=-1)
    return out.astype(x.dtype)


def causal_depthwise_conv(xbc_ext, w, b):
    c = xbc_ext.shape[-1]
    out = lax.conv_general_dilated(xbc_ext, w[:, None, :].astype(xbc_ext.dtype), window_strides=(1,), padding='VALID', dimension_numbers=('NWC', 'WIO', 'NWC'), feature_group_count=c)
    return out + b


def ssd_scan(x, dt, a, bm, cm, h0):
    b_, l_, g_, r_, p_ = x.shape
    n_ = bm.shape[-1]
    q_ = math.gcd(l_, SSD_CHUNK)
    nc = l_ // q_
    x = x.reshape(b_, nc, q_, g_, r_, p_)
    dt = dt.reshape(b_, nc, q_, g_, r_)
    bm = bm.reshape(b_, nc, q_, g_, n_)
    cm = cm.reshape(b_, nc, q_, g_, n_)
    acum = jnp.cumsum(dt * a, axis=2)
    causal = jnp.tril(jnp.ones((q_, q_), bool))[None, None, :, :, None, None]
    seg = acum[:, :, :, None] - acum[:, :, None, :]
    decay = jnp.exp(jnp.where(causal, seg, -jnp.inf))
    cb = jnp.einsum('bcign,bcjgn->bcijg', cm, bm)
    y_diag = jnp.einsum('bcijg,bcijgr,bcjgr,bcjgrp->bcigrp', cb, decay, dt, x)
    decay_end = jnp.exp(acum[:, :, -1:] - acum)
    states = jnp.einsum('bcjgn,bcjgr,bcjgrp->bcgrpn', bm, decay_end * dt, x)
    chunk_decay = jnp.exp(acum[:, :, -1])

    def step(h, inp):
        dec, st = inp
        return dec[..., None, None] * h + st, h

    h_final, h_start = lax.scan(step, h0, (jnp.moveaxis(chunk_decay, 1, 0), jnp.moveaxis(states, 1, 0)))
    h_start = jnp.moveaxis(h_start, 0, 1)
    y_off = jnp.einsum('bcign,bcigr,bcgrpn->bcigrp', cm, jnp.exp(acum), h_start)
    return (y_diag + y_off).reshape(b_, l_, g_, r_, p_), h_final


def ssd_mixer(h, conv_buf, ssm_state, w_in, conv_w, conv_b, dt_bias, a_log, d_skip, gate_norm, w_out):
    b_, l_, _ = h.shape
    f32 = jnp.float32
    proj = h @ w_in
    z = proj[..., :SSD_INNER]
    xbc = proj[..., SSD_INNER:SSD_INNER + SSD_CONV_DIM]
    dt_raw = proj[..., SSD_INNER + SSD_CONV_DIM:]
    xbc_ext = jnp.concatenate([conv_buf.astype(xbc.dtype), xbc], axis=1)
    new_conv = xbc_ext[:, -(SSD_CONV - 1):]
    xbc = jax.nn.silu(causal_depthwise_conv(xbc_ext, conv_w, conv_b))
    gn = SSD_GROUPS * SSD_STATE
    xs = xbc[..., :SSD_INNER].reshape(b_, l_, SSD_GROUPS, SSD_HPG, SSD_HEAD_DIM).astype(f32)
    bm = xbc[..., SSD_INNER:SSD_INNER + gn].reshape(b_, l_, SSD_GROUPS, SSD_STATE).astype(f32)
    cm = xbc[..., SSD_INNER + gn:].reshape(b_, l_, SSD_GROUPS, SSD_STATE).astype(f32)
    dt = jax.nn.softplus(dt_raw.astype(f32) + dt_bias.astype(f32)).reshape(b_, l_, SSD_GROUPS, SSD_HPG)
    a = -jnp.exp(a_log.astype(f32)).reshape(SSD_GROUPS, SSD_HPG)
    h0 = ssm_state.astype(f32).reshape(b_, SSD_GROUPS, SSD_HPG, SSD_HEAD_DIM, SSD_STATE)
    y, h_final = ssd_scan(xs, dt, a, bm, cm, h0)
    y = y + d_skip.astype(f32).reshape(SSD_GROUPS, SSD_HPG)[:, :, None] * xs
    g = y.reshape(b_, l_, SSD_GROUPS, SSD_HPG * SSD_HEAD_DIM) * jax.nn.silu(z.astype(f32)).reshape(b_, l_, SSD_GROUPS, SSD_HPG * SSD_HEAD_DIM)
    g = g * lax.rsqrt(jnp.mean(g * g, axis=-1, keepdims=True) + NORM_EPS)
    g = g.reshape(b_, l_, SSD_INNER) * gate_norm.astype(f32)
    out = g.astype(h.dtype) @ w_out
    return out, new_conv, h_final.reshape(b_, SSD_HEADS, SSD_HEAD_DIM, SSD_STATE).astype(ssm_state.dtype)


def moba_project(h, pos, w_qkv):
    b_, l_, _ = h.shape
    qkv = (h @ w_qkv).reshape(b_, l_, 3, MOBA_HEADS, MOBA_HEAD_DIM)
    q = partial_rotary(qkv[:, :, 0], pos)
    k = partial_rotary(qkv[:, :, 1], pos)
    return q, k, qkv[:, :, 2]


def key_blocks(k_all, v_all):
    b_, lk, h_, d_ = k_all.shape
    kb = k_all.reshape(b_, lk // MOBA_BLOCK, MOBA_BLOCK, h_, d_)
    vb = v_all.reshape(b_, lk // MOBA_BLOCK, MOBA_BLOCK, h_, d_)
    kmean = jnp.mean(kb, axis=2, dtype=jnp.float32)
    return kb, vb, kmean


def moba_select_attend(q, q_pos, kb, vb, kmean):
    b_, t_, h_, d_ = q.shape
    nb = kb.shape[1]
    cur = q_pos // MOBA_BLOCK
    gate = jnp.einsum('bthd,bnhd->bhtn', q.astype(jnp.float32), kmean)
    fully_past = jnp.arange(nb)[None, :] < cur[:, None]
    gate = jnp.where(fully_past[None, None], gate, -jnp.inf)
    _, top = lax.top_k(gate, min(MOBA_TOPK, nb))
    own = jnp.broadcast_to(cur[None, None, :, None], (b_, h_, t_, 1)).astype(jnp.int32)
    sel = jnp.concatenate([top.astype(jnp.int32), own], axis=-1)
    valid = jnp.concatenate([top < cur[None, None, :, None], jnp.ones(own.shape, bool)], axis=-1)
    kpos = sel[..., None] * MOBA_BLOCK + jnp.arange(MOBA_BLOCK)
    mask = valid[..., None] & (kpos <= q_pos[None, None, :, None, None])
    bi = jnp.arange(b_)[:, None, None, None]
    hi = jnp.arange(h_)[None, :, None, None]
    k_sel = kb[bi, sel, :, hi]
    v_sel = vb[bi, sel, :, hi]
    s = jnp.einsum('bthd,bhtnkd->bhtnk', q, k_sel, preferred_element_type=jnp.float32) * (d_ ** -0.5)
    s = jnp.where(mask, s, -jnp.inf).reshape(b_, h_, t_, -1)
    p = jax.nn.softmax(s, axis=-1).reshape(mask.shape)
    o = jnp.einsum('bhtnk,bhtnkd->bthd', p.astype(v_sel.dtype), v_sel, preferred_element_type=jnp.float32)
    return o.astype(q.dtype)


def moba_prompt(h, w_qkv, w_o):
    b_, l_, _ = h.shape
    pos = jnp.arange(l_, dtype=jnp.int32)
    q, k, v = moba_project(h, pos, w_qkv)
    pad = (-l_) % MOBA_BLOCK
    padw = ((0, 0), (0, pad), (0, 0), (0, 0))
    kb, vb, kmean = key_blocks(jnp.pad(k, padw), jnp.pad(v, padw))
    nq = l_ // MOBA_Q_BLOCK
    q_blocks = jnp.moveaxis(q.reshape(b_, nq, MOBA_Q_BLOCK, MOBA_HEADS, MOBA_HEAD_DIM), 1, 0)
    pos_blocks = pos.reshape(nq, MOBA_Q_BLOCK)
    o = lax.map(lambda qp: moba_select_attend(qp[0], qp[1], kb, vb, kmean), (q_blocks, pos_blocks))
    o = jnp.moveaxis(o, 0, 1).reshape(b_, l_, MOBA_HEADS * MOBA_HEAD_DIM)
    return o @ w_o, k, v


def moba_sample(h, cache_k, cache_v, page_table, w_qkv, w_o):
    b_, t_, _ = h.shape
    past = page_table.shape[1] * PAGE_SIZE
    pos = past + jnp.arange(t_, dtype=jnp.int32)
    q, k, v = moba_project(h, pos, w_qkv)
    past_k = cache_k[page_table].reshape(b_, past, MOBA_HEADS, MOBA_HEAD_DIM).astype(k.dtype)
    past_v = cache_v[page_table].reshape(b_, past, MOBA_HEADS, MOBA_HEAD_DIM).astype(v.dtype)
    pad = (-(past + t_)) % MOBA_BLOCK
    zeros = jnp.zeros((b_, pad, MOBA_HEADS, MOBA_HEAD_DIM), k.dtype)
    kb, vb, kmean = key_blocks(jnp.concatenate([past_k, k, zeros], axis=1), jnp.concatenate([past_v, v, zeros], axis=1))
    o = moba_select_attend(q, pos, kb, vb, kmean).reshape(b_, t_, MOBA_HEADS * MOBA_HEAD_DIM)
    return o @ w_o, k, v


def gmlp_mixer(h, w_in, ln_g, ln_b, w_s, b_s, w_out):
    b_, l_, _ = h.shape
    uv = jax.nn.gelu(h @ w_in, approximate=False)
    u = uv[..., :GMLP_DIM]
    vf = uv[..., GMLP_DIM:].astype(jnp.float32)
    mu = jnp.mean(vf, axis=-1, keepdims=True)
    var = jnp.mean(jnp.square(vf - mu), axis=-1, keepdims=True)
    v = ((vf - mu) * lax.rsqrt(var + NORM_EPS) * ln_g.astype(jnp.float32) + ln_b.astype(jnp.float32)).astype(h.dtype)
    n = min(l_, GMLP_CHUNK)
    nc = l_ // n
    w = jnp.tril(w_s[:, :n, :n])
    vc = v.reshape(b_, nc, n, GMLP_GROUPS, GMLP_GROUP_DIM)
    s = jnp.einsum('gij,bcjgd->bcigd', w, vc) + b_s[:, :n].T[None, None, :, :, None]
    gated = u.reshape(b_, nc, n, GMLP_GROUPS, GMLP_GROUP_DIM) * s
    return gated.reshape(b_, l_, GMLP_DIM) @ w_out, v


def setup_inputs(seed: int = 0) -> dict:
    key = jax.random.key(seed)
    ks = list(jax.random.split(key, 32))
    f32 = jnp.float32

    def nrm(k, shape, scale):
        return jax.random.normal(k, shape, f32) * scale

    n_pages = PAST_LEN // PAGE_SIZE
    n_pool = (5 * DEC_BATCH * n_pages) // 4
    page_table = jax.random.permutation(ks[0], n_pool)[:DEC_BATCH * n_pages].reshape(DEC_BATCH, n_pages).astype(jnp.int32)
    dt0 = jnp.exp(jax.random.uniform(ks[1], (N_SSD_LAYERS, SSD_HEADS), f32, math.log(1e-3), math.log(1e-1)))
    ssd_dt_bias = dt0 + jnp.log(-jnp.expm1(-dt0))
    ssd_a_log = jnp.log(jax.random.uniform(ks[2], (N_SSD_LAYERS, SSD_HEADS), f32, 1.0, 16.0))
    kv_shape = (N_MOBA_LAYERS, n_pool, PAGE_SIZE, MOBA_HEADS, MOBA_HEAD_DIM)
    attn_w = MOBA_HEADS * MOBA_HEAD_DIM
    return {
        'x_prompt': nrm(ks[3], (BATCH, SEQ, D_MODEL), 1.0),
        'x_sample': nrm(ks[4], (DEC_BATCH, DEC_SEQ, D_MODEL), 1.0),
        'state_ssm': nrm(ks[5], (N_SSD_LAYERS, DEC_BATCH, SSD_HEADS, SSD_HEAD_DIM, SSD_STATE), 0.1),
        'state_conv': nrm(ks[6], (N_SSD_LAYERS, DEC_BATCH, SSD_CONV - 1, SSD_CONV_DIM), 1.0),
        'cache_k': nrm(ks[7], kv_shape, 1.0),
        'cache_v': nrm(ks[8], kv_shape, 1.0),
        'page_table': page_table,
        'norm_mix': 1.0 + nrm(ks[9], (DEPTH, D_MODEL), 0.05),
        'norm_ffn': 1.0 + nrm(ks[10], (DEPTH, D_MODEL), 0.05),
        'norm_final': 1.0 + nrm(ks[11], (D_MODEL,), 0.05),
        'ssd_w_in': nrm(ks[12], (N_SSD_LAYERS, D_MODEL, SSD_IN_DIM), D_MODEL ** -0.5),
        'ssd_conv_w': nrm(ks[13], (N_SSD_LAYERS, SSD_CONV, SSD_CONV_DIM), SSD_CONV ** -0.5),
        'ssd_conv_b': nrm(ks[14], (N_SSD_LAYERS, SSD_CONV_DIM), 0.02),
        'ssd_dt_bias': ssd_dt_bias,
        'ssd_a_log': ssd_a_log,
        'ssd_d': 1.0 + nrm(ks[15], (N_SSD_LAYERS, SSD_HEADS), 0.1),
        'ssd_gate_norm': 1.0 + nrm(ks[16], (N_SSD_LAYERS, SSD_INNER), 0.05),
        'ssd_w_out': nrm(ks[17], (N_SSD_LAYERS, SSD_INNER, D_MODEL), SSD_INNER ** -0.5),
        'moba_w_qkv': nrm(ks[18], (N_MOBA_LAYERS, D_MODEL, 3 * attn_w), D_MODEL ** -0.5),
        'moba_w_o': nrm(ks[19], (N_MOBA_LAYERS, attn_w, D_MODEL), attn_w ** -0.5),
        'gmlp_w_in': nrm(ks[20], (N_GMLP_LAYERS, D_MODEL, 2 * GMLP_DIM), D_MODEL ** -0.5),
        'gmlp_ln_g': 1.0 + nrm(ks[21], (N_GMLP_LAYERS, GMLP_DIM), 0.05),
        'gmlp_ln_b': nrm(ks[22], (N_GMLP_LAYERS, GMLP_DIM), 0.02),
        'gmlp_w_s': nrm(ks[23], (N_GMLP_LAYERS, GMLP_GROUPS, GMLP_CHUNK, GMLP_CHUNK), GMLP_CHUNK ** -0.5),
        'gmlp_b_s': 1.0 + nrm(ks[24], (N_GMLP_LAYERS, GMLP_GROUPS, GMLP_CHUNK), 0.1),
        'gmlp_w_out': nrm(ks[25], (N_GMLP_LAYERS, GMLP_DIM, D_MODEL), GMLP_DIM ** -0.5),
        'ffn_w_gate': nrm(ks[26], (DEPTH, D_MODEL, FFN_HIDDEN), D_MODEL ** -0.5),
        'ffn_w_up': nrm(ks[27], (DEPTH, D_MODEL, FFN_HIDDEN), D_MODEL ** -0.5),
        'ffn_w_down': nrm(ks[28], (DEPTH, FFN_HIDDEN, D_MODEL), FFN_HIDDEN ** -0.5),
    }


def reference(x_prompt, x_sample, state_ssm, state_conv, cache_k, cache_v, page_table, norm_mix, norm_ffn, norm_final, ssd_w_in, ssd_conv_w, ssd_conv_b, ssd_dt_bias, ssd_a_log, ssd_d, ssd_gate_norm, ssd_w_out, moba_w_qkv, moba_w_o, gmlp_w_in, gmlp_ln_g, gmlp_ln_b, gmlp_w_s, gmlp_b_s, gmlp_w_out, ffn_w_gate, ffn_w_up, ffn_w_down):
    hp = x_prompt
    hs = x_sample
    bp = x_prompt.shape[0]
    ssm_p, ssm_s, conv_p, conv_s = [], [], [], []
    k_p, v_p, k_s, v_s, gv_s = [], [], [], [], []
    for i in range(DEPTH):
        kind = i % N_MIXERS
        j = i // N_MIXERS
        ap = rms_norm(hp, norm_mix[i])
        as_ = rms_norm(hs, norm_mix[i])
        if kind == 0:
            w = (ssd_w_in[j], ssd_conv_w[j], ssd_conv_b[j], ssd_dt_bias[j], ssd_a_log[j], ssd_d[j], ssd_gate_norm[j], ssd_w_out[j])
            zero_conv = jnp.zeros((bp, SSD_CONV - 1, SSD_CONV_DIM), state_conv.dtype)
            zero_ssm = jnp.zeros((bp, SSD_HEADS, SSD_HEAD_DIM, SSD_STATE), state_ssm.dtype)
            mp, cp_, sp_ = ssd_mixer(ap, zero_conv, zero_ssm, *w)
            ms, cs_, ss_ = ssd_mixer(as_, state_conv[j], state_ssm[j], *w)
            conv_p.append(cp_)
            ssm_p.append(sp_)
            conv_s.append(cs_)
            ssm_s.append(ss_)
        elif kind == 1:
            mp, kp_, vp_ = moba_prompt(ap, moba_w_qkv[j], moba_w_o[j])
            ms, ks_, vs_ = moba_sample(as_, cache_k[j], cache_v[j], page_table, moba_w_qkv[j], moba_w_o[j])
            k_p.append(kp_)
            v_p.append(vp_)
            k_s.append(ks_)
            v_s.append(vs_)
        else:
            w = (gmlp_w_in[j], gmlp_ln_g[j], gmlp_ln_b[j], gmlp_w_s[j], gmlp_b_s[j], gmlp_w_out[j])
            mp, _ = gmlp_mixer(ap, *w)
            ms, vrows = gmlp_mixer(as_, *w)
            gv_s.append(vrows)
        hp = hp + mp
        hs = hs + ms
        hp = hp + swiglu_ffn(rms_norm(hp, norm_ffn[i]), ffn_w_gate[i], ffn_w_up[i], ffn_w_down[i])
        hs = hs + swiglu_ffn(rms_norm(hs, norm_ffn[i]), ffn_w_gate[i], ffn_w_up[i], ffn_w_down[i])
    y_prompt = rms_norm(hp, norm_final)
    y_sample = rms_norm(hs, norm_final)
    return (y_prompt, y_sample, jnp.stack(ssm_p), jnp.stack(ssm_s), jnp.stack(conv_p), jnp.stack(conv_s), jnp.stack(k_p), jnp.stack(v_p), jnp.stack(k_s), jnp.stack(v_s), jnp.stack(gv_s))
```

```python
import functools
import math

import jax
import jax.numpy as jnp
from jax import lax
from jax.experimental import pallas as pl
from jax.experimental.pallas import tpu as pltpu

f32 = jnp.float32
bf16 = jnp.bfloat16
i32 = jnp.int32

D_MODEL = 2048
DEPTH = 4
PAGE_SIZE = 128
N_MIXERS = 3
NORM_EPS = 1e-6
SSD_INNER = 4096
SSD_HEAD_DIM = 64
SSD_HEADS = 64
SSD_GROUPS = 8
SSD_HPG = 8
SSD_STATE = 128
SSD_CONV = 4
SSD_CHUNK = 128
SSD_GROUP_W = SSD_HPG * SSD_HEAD_DIM
SSD_BC_W = 2 * SSD_GROUPS * SSD_STATE
SSD_CONV_DIM = SSD_INNER + SSD_BC_W
SSD_MAIN_W = SSD_INNER + SSD_CONV_DIM
MOBA_HEADS = 16
MOBA_HEAD_DIM = 128
MOBA_BLOCK = 256
MOBA_TOPK = 3
ROPE_THETA = 500000.0
ROPE_DIM = 32
ROPE_HALF = ROPE_DIM // 2
ATTN_W = MOBA_HEADS * MOBA_HEAD_DIM
GMLP_DIM = 2048
GMLP_GROUPS = 16
GMLP_GROUP_DIM = 128
GMLP_CHUNK = 128

V7X_LANES = 128
V7X_SUBLANES = 8
V7X_VMEM_LIMIT_BYTES = 60 * 1024 * 1024
MIB = 1024 * 1024


def _cparams(n_grid, vmem_bytes):
    return pltpu.CompilerParams(
        dimension_semantics=("arbitrary",) * n_grid,
        vmem_limit_bytes=int(min(max(vmem_bytes, 16 * MIB), V7X_VMEM_LIMIT_BYTES)),
    )


def _split3(x):
    hi = x.astype(bf16)
    r1 = x - hi.astype(f32)
    mid = r1.astype(bf16)
    lo = (r1 - mid.astype(f32)).astype(bf16)
    return hi, mid, lo


def _dot(a, b):
    return jnp.dot(a, b, preferred_element_type=f32)


def _dot_nt(a, b):
    return lax.dot_general(a, b, (((1,), (1,)), ((), ())), preferred_element_type=f32)


def _dot_tn(a, b):
    return lax.dot_general(a, b, (((0,), (0,)), ((), ())), preferred_element_type=f32)


def _gelu_exact(x):
    return 0.5 * x * (1.0 + lax.erf(x * (1.0 / math.sqrt(2.0))))


def _mm_body(*refs, n_w, has_gain, has_res, use_scratch, epi):
    it = iter(refs)
    x_ref = next(it)
    g_ref = next(it) if has_gain else None
    w_refs = [next(it) for _ in range(n_w)]
    r_ref = next(it) if has_res else None
    o_ref = next(it)
    xs_ref = next(it) if use_scratch else None

    if use_scratch:
        @pl.when(pl.program_id(1) == 0)
        def _():
            x = x_ref[...].astype(f32)
            if has_gain:
                x = x * lax.rsqrt(jnp.mean(x * x, axis=-1, keepdims=True) + NORM_EPS) * g_ref[...]
            xs_ref[...] = x.astype(bf16)
        xb = xs_ref[...]
    else:
        xb = x_ref[...]
    accs = [_dot(xb, w[...].astype(bf16)) for w in w_refs]
    if epi == "swiglu":
        out = jax.nn.silu(accs[0]) * accs[1]
    elif epi == "gelu":
        out = _gelu_exact(accs[0])
    else:
        out = accs[0]
    if has_res:
        out = r_ref[...] + out
    o_ref[...] = out.astype(o_ref.dtype)


def _mm(x, ws, *, n_out=None, gain=None, res=None, epi="none", out_dtype=f32, tn=512):
    m, k = x.shape
    n = n_out if n_out is not None else ws[0].shape[1]
    tm = next((t for t in (1024, 512, 256, 128, 64, 32, 16, 8) if m % t == 0), m)
    tn = min(tn, n)
    assert m % tm == 0 and n % tn == 0
    n_w = len(ws)
    has_gain = gain is not None
    has_res = res is not None
    use_scratch = has_gain or x.dtype != bf16
    in_specs = [pl.BlockSpec((tm, k), lambda i, j: (i, 0))]
    args = [x]
    if has_gain:
        in_specs.append(pl.BlockSpec((1, k), lambda i, j: (0, 0)))
        args.append(gain.reshape(1, k).astype(f32))
    for w in ws:
        in_specs.append(pl.BlockSpec((k, tn), lambda i, j: (0, j)))
        args.append(w)
    if has_res:
        in_specs.append(pl.BlockSpec((tm, tn), lambda i, j: (i, j)))
        args.append(res)
    scratch = [pltpu.VMEM((tm, k), bf16)] if use_scratch else []
    xbytes = x.dtype.itemsize
    est = (2 * tm * k * xbytes + (tm * k * 2 if use_scratch else 0) + n_w * k * tn * (2 * 4 + 2)
           + tm * tn * (2 * jnp.dtype(out_dtype).itemsize + (8 if has_res else 0) + 4 * n_w + 4) + 4 * MIB)
    body = functools.partial(_mm_body, n_w=n_w, has_gain=has_gain, has_res=has_res, use_scratch=use_scratch, epi=epi)
    return pl.pallas_call(
        body,
        grid=(m // tm, n // tn),
        in_specs=in_specs,
        out_specs=pl.BlockSpec((tm, tn), lambda i, j: (i, j)),
        out_shape=jax.ShapeDtypeStruct((m, n), out_dtype),
        scratch_shapes=scratch,
        compiler_params=_cparams(2, est),
        name="mm_" + epi,
    )(*args)


def _rms_body(x_ref, g_ref, o_ref):
    x = x_ref[...]
    o_ref[...] = x * lax.rsqrt(jnp.mean(x * x, axis=-1, keepdims=True) + NORM_EPS) * g_ref[...]


def _rms_norm(x, gain):
    m, d = x.shape
    tm = min(m, 512)
    return pl.pallas_call(
        _rms_body,
        grid=(m // tm,),
        in_specs=[pl.BlockSpec((tm, d), lambda i: (i, 0)), pl.BlockSpec((1, d), lambda i: (0, 0))],
        out_specs=pl.BlockSpec((tm, d), lambda i: (i, 0)),
        out_shape=jax.ShapeDtypeStruct((m, d), f32),
        compiler_params=_cparams(1, 4 * tm * d * 4 + 4 * MIB),
        name="final_rms",
    )(x, gain.reshape(1, d))


def _ssd_body(z_ref, xp_ref, bc_ref, dtq_ref, dtt_ref, cw_ref, cb_ref, dtb_ref, dtbt_ref, alog_ref, alogt_ref,
              dsk_ref, gn_ref, h0_ref, c0_ref, yg_ref, ssm_ref, ht_ref, xs_ref, xc_ref, *, valid_len):
    q = SSD_CHUNK
    c = pl.program_id(1)
    n_c = pl.num_programs(1)

    @pl.when(c == 0)
    def _():
        ht_ref[...] = h0_ref[0]
        xs_ref[0:V7X_SUBLANES, :] = c0_ref[0]

    xs_ref[V7X_SUBLANES:V7X_SUBLANES + q, 0:SSD_INNER] = xp_ref[0]
    xs_ref[V7X_SUBLANES:V7X_SUBLANES + q, SSD_INNER:SSD_CONV_DIM] = bc_ref[0]
    lane_step = 512
    for s in range(SSD_CONV_DIM // lane_step):
        sl = slice(s * lane_step, (s + 1) * lane_step)
        acc = cb_ref[:, sl] + cw_ref[SSD_CONV - 1:SSD_CONV, sl] * xs_ref[V7X_SUBLANES:V7X_SUBLANES + q, sl]
        for kk in range(1, SSD_CONV):
            acc = acc + cw_ref[SSD_CONV - 1 - kk:SSD_CONV - kk, sl] * xs_ref[V7X_SUBLANES - kk:V7X_SUBLANES - kk + q, sl]
        xc_ref[:, sl] = jax.nn.silu(acc)
    xs_ref[0:V7X_SUBLANES, :] = xs_ref[q:q + V7X_SUBLANES, :]

    ii = lax.broadcasted_iota(i32, (q, q), 0)
    jj = lax.broadcasted_iota(i32, (q, q), 1)
    causal = ii >= jj
    tril = jnp.where(causal, 1.0, 0.0).astype(bf16)
    triu = jnp.where(jj >= ii, 1.0, 0.0).astype(bf16)

    def group(g, carry):
        dt = jax.nn.softplus(dtq_ref[0, g] + dtb_ref[g])
        dtt = jax.nn.softplus(dtt_ref[0, g] + dtbt_ref[g])
        if valid_len is not None:
            dt = jnp.where(lax.broadcasted_iota(i32, dt.shape, 0) < valid_len, dt, 0.0)
            dtt = jnp.where(lax.broadcasted_iota(i32, dtt.shape, 1) < valid_len, dtt, 0.0)
        adt = dt * (-jnp.exp(alog_ref[g]))
        adtt = dtt * (-jnp.exp(alogt_ref[g]))
        acum = sum(_dot(tril, p) for p in _split3(adt))
        acumt = sum(_dot(p, triu) for p in _split3(adtt))
        last = acum[q - 1:q, :]

        xoff = pl.multiple_of(g * SSD_GROUP_W, SSD_GROUP_W)
        boff = pl.multiple_of(SSD_INNER + g * SSD_STATE, SSD_STATE)
        coff = pl.multiple_of(SSD_INNER + SSD_GROUPS * SSD_STATE + g * SSD_STATE, SSD_STATE)
        xg = xc_ref[:, pl.ds(xoff, SSD_GROUP_W)]
        bg = xc_ref[:, pl.ds(boff, SSD_STATE)].astype(bf16)
        cg = xc_ref[:, pl.ds(coff, SSD_STATE)].astype(bf16)
        cbm = _dot_nt(cg, bg)
        htg = ht_ref[:, pl.ds(xoff, SSD_GROUP_W)]
        yoff = _dot(cg, htg.astype(bf16))
        dsk = dsk_ref[g]

        ys, xws, dls = [], [], []
        for r in range(SSD_HPG):
            col = acum[:, r:r + 1]
            row = acumt[r:r + 1, :]
            dec = jnp.exp(jnp.where(causal, col - row, -jnp.inf))
            mm_ = cbm * dec * dtt[r:r + 1, :]
            xh = xg[:, r * SSD_HEAD_DIM:(r + 1) * SSD_HEAD_DIM]
            yd = _dot(mm_.astype(bf16), xh.astype(bf16))
            yo = yoff[:, r * SSD_HEAD_DIM:(r + 1) * SSD_HEAD_DIM] * jnp.exp(col)
            ys.append(yd + yo + dsk[:, r:r + 1] * xh)
            xws.append(xh * (jnp.exp(last[:, r:r + 1] - col) * dt[:, r:r + 1]))
            dls.append(jnp.broadcast_to(jnp.exp(last[:, r:r + 1]), (1, SSD_HEAD_DIM)))
        yg = jnp.concatenate(ys, axis=1)
        xw = jnp.concatenate(xws, axis=1)
        dl = jnp.concatenate(dls, axis=1)

        ht_ref[:, pl.ds(xoff, SSD_GROUP_W)] = htg * dl + _dot_tn(bg, xw.astype(bf16))

        gz = yg * jax.nn.silu(z_ref[0, :, pl.ds(xoff, SSD_GROUP_W)])
        gzn = gz * lax.rsqrt(jnp.mean(gz * gz, axis=-1, keepdims=True) + NORM_EPS)
        yg_ref[0, :, pl.ds(xoff, SSD_GROUP_W)] = (gzn * gn_ref[:, pl.ds(xoff, SSD_GROUP_W)]).astype(yg_ref.dtype)
        return carry

    lax.fori_loop(0, SSD_GROUPS, group, 0)

    @pl.when(c == n_c - 1)
    def _():
        for g in range(SSD_GROUPS):
            sl = slice(g * SSD_GROUP_W, (g + 1) * SSD_GROUP_W)
            ssm_ref[0, sl, :] = ht_ref[:, sl].T


def _ssd_scan(proj, dt_raw, conv0, ssm0, conv_w, conv_b, dt_bias, a_log, d_skip, gate_norm, *, valid_len=None):
    b_, l_, _ = proj.shape
    q = SSD_CHUNK
    assert l_ % q == 0
    n_c = l_ // q
    g_, r_ = SSD_GROUPS, SSD_HPG
    dtq = dt_raw.reshape(b_, l_, g_, r_).transpose(0, 2, 1, 3)
    dtt = dt_raw.reshape(b_, l_, g_, r_).transpose(0, 2, 3, 1)
    h0t = ssm0.reshape(b_, SSD_INNER, SSD_STATE).transpose(0, 2, 1)
    c0 = jnp.pad(conv0, ((0, 0), (V7X_SUBLANES - (SSD_CONV - 1), 0), (0, 0)))
    dtb = dt_bias.reshape(g_, 1, r_)
    dtbt = dt_bias.reshape(g_, r_, 1)
    alog = a_log.reshape(g_, 1, r_)
    alogt = a_log.reshape(g_, r_, 1)
    dsk = d_skip.reshape(g_, 1, r_)
    full = lambda shape: pl.BlockSpec(shape, lambda b, c: (0,) * len(shape))
    in_specs = [
        pl.BlockSpec((1, q, SSD_INNER), lambda b, c: (b, c, 0)),
        pl.BlockSpec((1, q, SSD_INNER), lambda b, c: (b, c, 1)),
        pl.BlockSpec((1, q, SSD_BC_W), lambda b, c: (b, c, 2 * SSD_INNER // SSD_BC_W)),
        pl.BlockSpec((1, g_, q, r_), lambda b, c: (b, 0, c, 0)),
        pl.BlockSpec((1, g_, r_, q), lambda b, c: (b, 0, 0, c)),
        full((SSD_CONV, SSD_CONV_DIM)), full((1, SSD_CONV_DIM)),
        full((g_, 1, r_)), full((g_, r_, 1)), full((g_, 1, r_)), full((g_, r_, 1)), full((g_, 1, r_)),
        full((1, SSD_INNER)),
        pl.BlockSpec((1, SSD_STATE, SSD_INNER), lambda b, c: (b, 0, 0)),
        pl.BlockSpec((1, V7X_SUBLANES, SSD_CONV_DIM), lambda b, c: (b, 0, 0)),
    ]
    out_specs = [
        pl.BlockSpec((1, q, SSD_INNER), lambda b, c: (b, c, 0)),
        pl.BlockSpec((1, SSD_INNER, SSD_STATE), lambda b, c: (b, 0, 0)),
    ]
    yg, ssm = pl.pallas_call(
        functools.partial(_ssd_body, valid_len=valid_len),
        grid=(b_, n_c),
        in_specs=in_specs,
        out_specs=out_specs,
        out_shape=[jax.ShapeDtypeStruct((b_, l_, SSD_INNER), bf16),
                   jax.ShapeDtypeStruct((b_, SSD_INNER, SSD_STATE), f32)],
        scratch_shapes=[pltpu.VMEM((SSD_STATE, SSD_INNER), f32),
                        pltpu.VMEM((q + 2 * V7X_SUBLANES, SSD_CONV_DIM), f32),
                        pltpu.VMEM((q, SSD_CONV_DIM), f32)],
        compiler_params=_cparams(2, 48 * MIB),
        name="ssd_scan",
    )(proj, proj, proj, dtq, dtt, conv_w, conv_b.reshape(1, -1), dtb, dtbt, alog, alogt, dsk,
      gate_norm.reshape(1, -1), h0t, c0)
    return yg, ssm.reshape(b_, SSD_HEADS, SSD_HEAD_DIM, SSD_STATE)


def _trig_body(pos_ref, invf_ref, sgn_ref, cos_ref, sin_ref):
    ang = pos_ref[...].astype(f32) * invf_ref[...]
    rot = lax.broadcasted_iota(i32, ang.shape, 1) < ROPE_DIM
    cos_ref[...] = jnp.where(rot, jnp.cos(ang), 1.0)
    sin_ref[...] = jnp.sin(ang) * sgn_ref[...]


def _trig_tables(pos):
    n = pos.shape[0]
    inv_freq = ROPE_THETA ** (-jnp.arange(ROPE_HALF, dtype=f32) * 2.0 / ROPE_DIM)
    pad = jnp.zeros((MOBA_HEAD_DIM - ROPE_DIM,), f32)
    invf = jnp.concatenate([inv_freq, inv_freq, pad]).reshape(1, MOBA_HEAD_DIM)
    sgn = jnp.concatenate([-jnp.ones((ROPE_HALF,), f32), jnp.ones((ROPE_HALF,), f32), pad]).reshape(1, MOBA_HEAD_DIM)
    out = jax.ShapeDtypeStruct((n, MOBA_HEAD_DIM), f32)
    return pl.pallas_call(_trig_body, out_shape=[out, out], name="rope_tables")(pos.reshape(n, 1).astype(i32), invf, sgn)


def _rotate(x, cos_t, sin_t):
    lane = lax.broadcasted_iota(i32, x.shape, 1)
    partner = jnp.where(lane < ROPE_HALF, pltpu.roll(x, MOBA_HEAD_DIM - ROPE_HALF, 1), pltpu.roll(x, ROPE_HALF, 1))
    return x * cos_t + partner * sin_t


def _prep_body(qkv_ref, cos_ref, sin_ref, q_ref, k_ref, kb_ref, vb_ref, *maybe_kmean, with_kmean):
    cos_t = cos_ref[...]
    sin_t = sin_ref[...]
    for h in range(MOBA_HEADS):
        sl = slice(h * MOBA_HEAD_DIM, (h + 1) * MOBA_HEAD_DIM)
        q_ref[:, sl] = _rotate(qkv_ref[:, sl], cos_t, sin_t)
        kr = _rotate(qkv_ref[:, ATTN_W + h * MOBA_HEAD_DIM:ATTN_W + (h + 1) * MOBA_HEAD_DIM], cos_t, sin_t)
        k_ref[:, sl] = kr
        kb_ref[:, sl] = kr.astype(bf16)
        if with_kmean:
            maybe_kmean[0][0, :, sl] = jnp.sum(kr, axis=0, keepdims=True) * (1.0 / MOBA_BLOCK)
    vb_ref[...] = qkv_ref[:, 2 * ATTN_W:3 * ATTN_W].astype(bf16)


def _moba_prep(qkv, cos_t, sin_t, *, rows, with_kmean):
    m = qkv.shape[0]
    n_tab = cos_t.shape[0] // rows
    row_spec = lambda w: pl.BlockSpec((rows, w), lambda i: (i, 0))
    tab_spec = pl.BlockSpec((rows, MOBA_HEAD_DIM), lambda i: (i % n_tab, 0))
    out_shape = [jax.ShapeDtypeStruct((m, ATTN_W), f32), jax.ShapeDtypeStruct((m, ATTN_W), f32),
                 jax.ShapeDtypeStruct((m, ATTN_W), bf16), jax.ShapeDtypeStruct((m, ATTN_W), bf16)]
    out_specs = [row_spec(ATTN_W)] * 4
    if with_kmean:
        out_shape.append(jax.ShapeDtypeStruct((m // rows, 1, ATTN_W), f32))
        out_specs.append(pl.BlockSpec((1, 1, ATTN_W), lambda i: (i, 0, 0)))
    return pl.pallas_call(
        functools.partial(_prep_body, with_kmean=with_kmean),
        grid=(m // rows,),
        in_specs=[row_spec(3 * ATTN_W), tab_spec, tab_spec],
        out_specs=out_specs,
        out_shape=out_shape,
        compiler_params=_cparams(1, 40 * MIB),
        name="moba_prep",
    )(qkv, cos_t, sin_t)


def _gate_scores(q, km):
    qh, qm, _ = _split3(q)
    kh, kmid, _ = _split3(km)
    return _dot_nt(qh, kh) + _dot_nt(qh, kmid) + _dot_nt(qm, kh)


def _top_blocks(gate, n_take):
    lane = lax.broadcasted_iota(i32, gate.shape, 1)
    picks = []
    for _ in range(n_take):
        mx = jnp.max(gate, axis=-1, keepdims=True)
        idx = jnp.min(jnp.where(gate == mx, lane, gate.shape[1]), axis=-1, keepdims=True)
        picks.append(idx)
        gate = jnp.where(lane == idx, -jnp.inf, gate)
    return picks


def _attn_prompt_body(q_ref, kb_ref, vb_ref, km_ref, o_ref):
    blk = MOBA_BLOCK
    cur = pl.program_id(2)
    n_blk = km_ref.shape[1]
    q = q_ref[...]
    qb = q.astype(bf16)
    scale = MOBA_HEAD_DIM ** -0.5

    lane = lax.broadcasted_iota(i32, (blk, n_blk), 1)
    past = lane < cur
    gate = jnp.where(past, _gate_scores(q, km_ref[0]), -jnp.inf)
    sel = jnp.zeros((blk, n_blk), f32)
    for idx in _top_blocks(gate, min(MOBA_TOPK, n_blk)):
        sel = jnp.where((lane == idx) & past, 1.0, sel)

    own = pl.multiple_of(cur * blk, blk)
    s = _dot_nt(qb, kb_ref[pl.ds(own, blk), :]) * scale
    ii = lax.broadcasted_iota(i32, (blk, blk), 0)
    jj = lax.broadcasted_iota(i32, (blk, blk), 1)
    s = jnp.where(jj <= ii, s, -jnp.inf)
    m0 = jnp.max(s, axis=-1, keepdims=True)
    p = jnp.exp(s - m0)
    l0 = jnp.sum(p, axis=-1, keepdims=True)
    acc0 = _dot(p.astype(bf16), vb_ref[pl.ds(own, blk), :])

    def body(n, carry):
        m_i, l_i, acc = carry
        off = pl.multiple_of(n * blk, blk)
        chosen = jnp.sum(jnp.where(lane == n, sel, 0.0), axis=-1, keepdims=True) > 0.0
        sn = _dot_nt(qb, kb_ref[pl.ds(off, blk), :]) * scale
        sn = jnp.where(chosen, sn, -jnp.inf)
        m_new = jnp.maximum(m_i, jnp.max(sn, axis=-1, keepdims=True))
        alpha = jnp.exp(m_i - m_new)
        pn = jnp.exp(sn - m_new)
        l_new = alpha * l_i + jnp.sum(pn, axis=-1, keepdims=True)
        acc = alpha * acc + _dot(pn.astype(bf16), vb_ref[pl.ds(off, blk), :])
        return m_new, l_new, acc

    _, l_f, acc_f = lax.fori_loop(0, cur, body, (m0, l0, acc0))
    o_ref[...] = (acc_f / l_f).astype(o_ref.dtype)


def _moba_attn_prompt(q, kb, vb, kmean, b_, l_):
    n_blk = l_ // MOBA_BLOCK
    return pl.pallas_call(
        _attn_prompt_body,
        grid=(b_, MOBA_HEADS, n_blk),
        in_specs=[
            pl.BlockSpec((MOBA_BLOCK, MOBA_HEAD_DIM), lambda b, h, t: (b * n_blk + t, h)),
            pl.BlockSpec((l_, MOBA_HEAD_DIM), lambda b, h, t: (b, h)),
            pl.BlockSpec((l_, MOBA_HEAD_DIM), lambda b, h, t: (b, h)),
            pl.BlockSpec((1, n_blk, MOBA_HEAD_DIM), lambda b, h, t: (b, 0, h)),
        ],
        out_specs=pl.BlockSpec((MOBA_BLOCK, MOBA_HEAD_DIM), lambda b, h, t: (b * n_blk + t, h)),
        out_shape=jax.ShapeDtypeStruct((b_ * l_, ATTN_W), bf16),
        compiler_params=_cparams(3, 24 * MIB),
        name="moba_attn_prompt",
    )(q, kb, vb, kmean)


PAGES_PER_BLOCK = MOBA_BLOCK // PAGE_SIZE
KMEAN_BLOCKS_PER_STEP = 8
KMEAN_PAGES_PER_STEP = KMEAN_BLOCKS_PER_STEP * PAGES_PER_BLOCK


def _kmean_body(pt_ref, *refs):
    page_refs, o_ref = refs[:-1], refs[-1]
    for blk in range(KMEAN_BLOCKS_PER_STEP):
        tot = None
        for p in range(PAGES_PER_BLOCK):
            part = jnp.sum(page_refs[blk * PAGES_PER_BLOCK + p][0], axis=0, keepdims=True)
            tot = part if tot is None else tot + part
        o_ref[0, blk:blk + 1, :] = tot * (1.0 / MOBA_BLOCK)


def _cache_kmean(cache2d, page_table):
    b_, n_pages = page_table.shape
    n_steps = n_pages // KMEAN_PAGES_PER_STEP
    pt_flat = page_table.reshape(-1)

    def page_spec(p):
        return pl.BlockSpec((1, PAGE_SIZE, ATTN_W),
                            lambda b, s, pt: (pt[b * n_pages + s * KMEAN_PAGES_PER_STEP + p], 0, 0))

    return pl.pallas_call(
        _kmean_body,
        grid_spec=pltpu.PrefetchScalarGridSpec(
            num_scalar_prefetch=1,
            grid=(b_, n_steps),
            in_specs=[page_spec(p) for p in range(KMEAN_PAGES_PER_STEP)],
            out_specs=pl.BlockSpec((1, KMEAN_BLOCKS_PER_STEP, ATTN_W), lambda b, s, pt: (b, s, 0)),
        ),
        out_shape=jax.ShapeDtypeStruct((b_, n_pages // PAGES_PER_BLOCK, ATTN_W), f32),
        compiler_params=_cparams(2, 40 * MIB),
        name="cache_kmean",
    )(pt_flat, *([cache2d] * KMEAN_PAGES_PER_STEP))


def _sample_topk_body(q_ref, km_ref, o_ref, g_ref):
    rows = q_ref.shape[1]
    for h in range(MOBA_HEADS):
        sl = slice(h * MOBA_HEAD_DIM, (h + 1) * MOBA_HEAD_DIM)
        g_ref[h * rows:(h + 1) * rows, :] = _gate_scores(q_ref[0, :, sl], km_ref[0, :, sl])
    picks = _top_blocks(g_ref[...], MOBA_TOPK)
    lane = lax.broadcasted_iota(i32, o_ref.shape[1:], 1)
    out = jnp.zeros(o_ref.shape[1:], i32)
    for s, idx in enumerate(picks):
        out = jnp.where(lane == s, idx, out)
    o_ref[0] = out


def _sample_topk(q8, kmean):
    b_, rows, _ = q8.shape
    n_blk = kmean.shape[1]
    return pl.pallas_call(
        _sample_topk_body,
        grid=(b_,),
        in_specs=[pl.BlockSpec((1, rows, ATTN_W), lambda b: (b, 0, 0)),
                  pl.BlockSpec((1, n_blk, ATTN_W), lambda b: (b, 0, 0))],
        out_specs=pl.BlockSpec((1, MOBA_HEADS * rows, V7X_LANES), lambda b: (b, 0, 0)),
        out_shape=jax.ShapeDtypeStruct((b_, MOBA_HEADS * rows, V7X_LANES), i32),
        scratch_shapes=[pltpu.VMEM((MOBA_HEADS * rows, n_blk), f32)],
        compiler_params=_cparams(1, 16 * MIB),
        name="sample_topk",
    )(q8, kmean)


def _attn_sample_body(sel_ref, pt_ref, q_ref, kn_ref, vn_ref, *refs, n_tok):
    n_sel = n_tok * MOBA_TOPK * PAGES_PER_BLOCK
    k_refs, v_refs, o_ref = refs[:n_sel], refs[n_sel:2 * n_sel], refs[-1]
    rows = q_ref.shape[1]
    scale = MOBA_HEAD_DIM ** -0.5
    qb = q_ref[0].astype(bf16)
    row = lax.broadcasted_iota(i32, (rows, PAGE_SIZE), 0)
    scores = []
    for i in range(n_sel):
        t = i // (MOBA_TOPK * PAGES_PER_BLOCK)
        s = _dot_nt(qb, k_refs[i][0].astype(bf16)) * scale
        scores.append(jnp.where(row == t, s, -jnp.inf))
    r8 = lax.broadcasted_iota(i32, (rows, rows), 0)
    c8 = lax.broadcasted_iota(i32, (rows, rows), 1)
    s_own = _dot_nt(qb, kn_ref[0]) * scale
    s_own = jnp.where((c8 <= r8) & (c8 < n_tok), s_own, -jnp.inf)
    m = jnp.max(s_own, axis=-1, keepdims=True)
    for s in scores:
        m = jnp.maximum(m, jnp.max(s, axis=-1, keepdims=True))
    m = jnp.where(m == -jnp.inf, 0.0, m)
    p_own = jnp.exp(s_own - m)
    l = jnp.sum(p_own, axis=-1, keepdims=True)
    acc = _dot(p_own.astype(bf16), vn_ref[0])
    for i in range(n_sel):
        p = jnp.exp(scores[i] - m)
        l = l + jnp.sum(p, axis=-1, keepdims=True)
        acc = acc + _dot(p.astype(bf16), v_refs[i][0].astype(bf16))
    l = jnp.where(l == 0.0, 1.0, l)
    o_ref[0] = (acc / l).astype(o_ref.dtype)


def _moba_attn_sample(sel, page_table, q8, kn8, vn8, ck2d, cv2d, n_tok):
    b_, rows, _ = q8.shape
    n_pages = page_table.shape[1]
    sel_flat = sel.reshape(-1).astype(i32)
    pt_flat = page_table.reshape(-1).astype(i32)
    head_spec = pl.BlockSpec((1, rows, MOBA_HEAD_DIM), lambda b, h, sel_r, pt_r: (b, 0, h))

    def page_spec(t, s, p):
        def imap(b, h, sel_r, pt_r):
            blk = sel_r[((b * MOBA_HEADS + h) * n_tok + t) * MOBA_TOPK + s]
            return (pt_r[b * n_pages + blk * PAGES_PER_BLOCK + p], 0, h)
        return pl.BlockSpec((1, PAGE_SIZE, MOBA_HEAD_DIM), imap)

    page_specs = [page_spec(t, s, p) for t in range(n_tok) for s in range(MOBA_TOPK) for p in range(PAGES_PER_BLOCK)]
    n_sel = len(page_specs)
    return pl.pallas_call(
        functools.partial(_attn_sample_body, n_tok=n_tok),
        grid_spec=pltpu.PrefetchScalarGridSpec(
            num_scalar_prefetch=2,
            grid=(b_, MOBA_HEADS),
            in_specs=[head_spec, head_spec, head_spec] + page_specs + page_specs,
            out_specs=head_spec,
        ),
        out_shape=jax.ShapeDtypeStruct((b_, rows, ATTN_W), bf16),
        compiler_params=_cparams(2, 24 * MIB),
        name="moba_attn_sample",
    )(sel_flat, pt_flat, q8, kn8, vn8, *([ck2d] * n_sel), *([cv2d] * n_sel))


def _gmlp_body(uv_ref, lg_ref, lb_ref, ws_ref, bst_ref, o_ref, *maybe_v, with_v):
    n = GMLP_CHUNK
    vf = uv_ref[:, GMLP_DIM:2 * GMLP_DIM]
    mu = jnp.mean(vf, axis=-1, keepdims=True)
    var = jnp.mean(jnp.square(vf - mu), axis=-1, keepdims=True)
    v = (vf - mu) * lax.rsqrt(var + NORM_EPS) * lg_ref[...] + lb_ref[...]
    if with_v:
        maybe_v[0][...] = v
    ii = lax.broadcasted_iota(i32, (n, n), 0)
    jj = lax.broadcasted_iota(i32, (n, n), 1)
    for g in range(GMLP_GROUPS):
        sl = slice(g * GMLP_GROUP_DIM, (g + 1) * GMLP_GROUP_DIM)
        w = jnp.where(jj <= ii, ws_ref[g], 0.0).astype(bf16)
        s = _dot(w, v[:, sl].astype(bf16)) + bst_ref[:, g:g + 1]
        o_ref[:, sl] = (uv_ref[:, sl] * s).astype(o_ref.dtype)


def _gmlp_gate(uv, ln_g, ln_b, w_s, b_s, *, with_v):
    m = uv.shape[0]
    n = GMLP_CHUNK
    out_shape = [jax.ShapeDtypeStruct((m, GMLP_DIM), bf16)]
    out_specs = [pl.BlockSpec((n, GMLP_DIM), lambda i: (i, 0))]
    if with_v:
        out_shape.append(jax.ShapeDtypeStruct((m, GMLP_DIM), f32))
        out_specs.append(pl.BlockSpec((n, GMLP_DIM), lambda i: (i, 0)))
    return pl.pallas_call(
        functools.partial(_gmlp_body, with_v=with_v),
        grid=(m // n,),
        in_specs=[pl.BlockSpec((n, 2 * GMLP_DIM), lambda i: (i, 0)),
                  pl.BlockSpec((1, GMLP_DIM), lambda i: (0, 0)),
                  pl.BlockSpec((1, GMLP_DIM), lambda i: (0, 0)),
                  pl.BlockSpec((GMLP_GROUPS, n, n), lambda i: (0, 0, 0)),
                  pl.BlockSpec((n, GMLP_GROUPS), lambda i: (0, 0))],
        out_specs=out_specs,
        out_shape=out_shape,
        compiler_params=_cparams(1, 24 * MIB),
        name="gmlp_gate",
    )(uv, ln_g.reshape(1, -1), ln_b.reshape(1, -1), w_s, b_s.T)


def _pad_rows(x, rows):
    return jnp.pad(x, ((0, 0), (0, rows - x.shape[1]), (0, 0)))


def _ssd_layer(h, b_, l_, gain, conv0, ssm0, w_in, conv_w, conv_b, dt_bias, a_log, d_skip, gate_norm, w_out):
    proj = _mm(h, [w_in], n_out=SSD_MAIN_W, gain=gain)
    dt_raw = _mm(h, [w_in[:, SSD_MAIN_W:]], gain=gain)
    proj3 = proj.reshape(b_, l_, SSD_MAIN_W)
    dt3 = dt_raw.reshape(b_, l_, SSD_HEADS)
    xbc_ext = jnp.concatenate([conv0, proj3[:, :, SSD_INNER:]], axis=1) if l_ < SSD_CONV - 1 else None
    if xbc_ext is not None:
        new_conv = xbc_ext[:, -(SSD_CONV - 1):]
    else:
        new_conv = proj3[:, l_ - (SSD_CONV - 1):, SSD_INNER:]
    lp = -(-l_ // SSD_CHUNK) * SSD_CHUNK
    valid_len = None
    if lp != l_:
        assert lp == SSD_CHUNK
        proj3 = _pad_rows(proj3, lp)
        dt3 = _pad_rows(dt3, lp)
        valid_len = l_
    yg, ssm = _ssd_scan(proj3, dt3, conv0, ssm0, conv_w, conv_b, dt_bias, a_log, d_skip, gate_norm, valid_len=valid_len)
    if lp != l_:
        yg = yg[:, :l_]
    h = _mm(yg.reshape(b_ * l_, SSD_INNER), [w_out], res=h)
    return h, new_conv, ssm


def _ffn_layer(h, gain, w_gate, w_up, w_down):
    t = _mm(h, [w_gate, w_up], gain=gain, epi="swiglu", out_dtype=bf16)
    return _mm(t, [w_down], res=h, tn=256)


def kernel(x_prompt, x_sample, state_ssm, state_conv, cache_k, cache_v, page_table, norm_mix, norm_ffn, norm_final, ssd_w_in, ssd_conv_w, ssd_conv_b, ssd_dt_bias, ssd_a_log, ssd_d, ssd_gate_norm, ssd_w_out, moba_w_qkv, moba_w_o, gmlp_w_in, gmlp_ln_g, gmlp_ln_b, gmlp_w_s, gmlp_b_s, gmlp_w_out, ffn_w_gate, ffn_w_up, ffn_w_down):
    bp, lp_, d = x_prompt.shape
    bs, ls, _ = x_sample.shape
    hp = x_prompt.reshape(bp * lp_, d)
    hs = x_sample.reshape(bs * ls, d)
    n_pages = page_table.shape[1]
    past = n_pages * PAGE_SIZE
    tok_rows = V7X_SUBLANES
    assert ls <= tok_rows and past % MOBA_BLOCK == 0 and lp_ % MOBA_BLOCK == 0

    ssm_p, ssm_s, conv_p, conv_s = [], [], [], []
    k_p, v_p, k_s, v_s, gv_s = [], [], [], [], []
    for i in range(DEPTH):
        kind = i % N_MIXERS
        j = i // N_MIXERS
        if kind == 0:
            w = (ssd_w_in[j], ssd_conv_w[j], ssd_conv_b[j], ssd_dt_bias[j], ssd_a_log[j], ssd_d[j], ssd_gate_norm[j], ssd_w_out[j])
            zero_conv = jnp.zeros((bp, SSD_CONV - 1, SSD_CONV_DIM), f32)
            zero_ssm = jnp.zeros((bp, SSD_HEADS, SSD_HEAD_DIM, SSD_STATE), f32)
            hp, cp_, sp_ = _ssd_layer(hp, bp, lp_, norm_mix[i], zero_conv, zero_ssm, *w)
            hs, cs_, ss_ = _ssd_layer(hs, bs, ls, norm_mix[i], state_conv[j], state_ssm[j], *w)
            conv_p.append(cp_)
            ssm_p.append(sp_)
            conv_s.append(cs_)
            ssm_s.append(ss_)
        elif kind == 1:
            qkv = _mm(hp, [moba_w_qkv[j]], gain=norm_mix[i])
            cos_t, sin_t = _trig_tables(jnp.arange(lp_, dtype=i32))
            q, k, kb, vb, kmean = _moba_prep(qkv, cos_t, sin_t, rows=MOBA_BLOCK, with_kmean=True)
            o = _moba_attn_prompt(q, kb, vb, kmean.reshape(bp, lp_ // MOBA_BLOCK, ATTN_W), bp, lp_)
            hp = _mm(o, [moba_w_o[j]], res=hp)
            k_p.append(k.reshape(bp, lp_, MOBA_HEADS, MOBA_HEAD_DIM))
            v_p.append(qkv[:, 2 * ATTN_W:].reshape(bp, lp_, MOBA_HEADS, MOBA_HEAD_DIM))
            qkv_s = _mm(hs, [moba_w_qkv[j]], gain=norm_mix[i])
            pos_s = past + (jnp.arange(bs * ls, dtype=i32) % ls)
            cos_s, sin_s = _trig_tables(pos_s)
            q_s, kn, knb, vnb = _moba_prep(qkv_s, cos_s, sin_s, rows=bs * ls, with_kmean=False)
            ck2d = cache_k[j].reshape(-1, PAGE_SIZE, ATTN_W)
            cv2d = cache_v[j].reshape(-1, PAGE_SIZE, ATTN_W)
            kmean_c = _cache_kmean(ck2d, page_table)
            q8 = _pad_rows(q_s.reshape(bs, ls, ATTN_W), tok_rows)
            kn8 = _pad_rows(knb.reshape(bs, ls, ATTN_W), tok_rows)
            vn8 = _pad_rows(vnb.reshape(bs, ls, ATTN_W), tok_rows)
            top = _sample_topk(q8, kmean_c)
            sel = top.reshape(bs, MOBA_HEADS, tok_rows, V7X_LANES)[:, :, :ls, :MOBA_TOPK]
            o_s = _moba_attn_sample(sel, page_table, q8, kn8, vn8, ck2d, cv2d, ls)
            hs = _mm(o_s[:, :ls].reshape(bs * ls, ATTN_W), [moba_w_o[j]], res=hs)
            k_s.append(kn.reshape(bs, ls, MOBA_HEADS, MOBA_HEAD_DIM))
            v_s.append(qkv_s[:, 2 * ATTN_W:].reshape(bs, ls, MOBA_HEADS, MOBA_HEAD_DIM))
        else:
            w = (gmlp_ln_g[j], gmlp_ln_b[j], gmlp_w_s[j], gmlp_b_s[j])
            uv = _mm(hp, [gmlp_w_in[j]], gain=norm_mix[i], epi="gelu")
            (gated,) = _gmlp_gate(uv, *w, with_v=False)
            hp = _mm(gated, [gmlp_w_out[j]], res=hp)
            uv_s = _mm(hs, [gmlp_w_in[j]], gain=norm_mix[i], epi="gelu")
            uv_s = _pad_rows(uv_s.reshape(bs, ls, 2 * GMLP_DIM), GMLP_CHUNK).reshape(bs * GMLP_CHUNK, 2 * GMLP_DIM)
            gated_s, v_rows = _gmlp_gate(uv_s, *w, with_v=True)
            gated_s = gated_s.reshape(bs, GMLP_CHUNK, GMLP_DIM)[:, :ls].reshape(bs * ls, GMLP_DIM)
            hs = _mm(gated_s, [gmlp_w_out[j]], res=hs)
            gv_s.append(v_rows.reshape(bs, GMLP_CHUNK, GMLP_DIM)[:, :ls])
        hp = _ffn_layer(hp, norm_ffn[i], ffn_w_gate[i], ffn_w_up[i], ffn_w_down[i])
        hs = _ffn_layer(hs, norm_ffn[i], ffn_w_gate[i], ffn_w_up[i], ffn_w_down[i])
    y_prompt = _rms_norm(hp, norm_final).reshape(bp, lp_, d)
    y_sample = _rms_norm(hs, norm_final).reshape(bs, ls, d)
    return (y_prompt, y_sample, jnp.stack(ssm_p), jnp.stack(ssm_s), jnp.stack(conv_p), jnp.stack(conv_s),
            jnp.stack(k_p), jnp.stack(v_p), jnp.stack(k_s), jnp.stack(v_s), jnp.stack(gv_s))
```

```python
import functools
import math

import jax
import jax.numpy as jnp
from jax import lax
from jax.experimental import pallas as pl
from jax.experimental.pallas import tpu as pltpu

f32 = jnp.float32
bf16 = jnp.bfloat16
i32 = jnp.int32

D_MODEL = 2048
DEPTH = 4
PAGE_SIZE = 128
N_MIXERS = 3
NORM_EPS = 1e-6
SSD_INNER = 4096
SSD_HEAD_DIM = 64
SSD_HEADS = 64
SSD_GROUPS = 8
SSD_HPG = 8
SSD_STATE = 128
SSD_CONV = 4
SSD_CHUNK = 128
SSD_GROUP_W = SSD_HPG * SSD_HEAD_DIM
SSD_BC_W = 2 * SSD_GROUPS * SSD_STATE
SSD_CONV_DIM = SSD_INNER + SSD_BC_W
SSD_MAIN_W = SSD_INNER + SSD_CONV_DIM
MOBA_HEADS = 16
MOBA_HEAD_DIM = 128
MOBA_BLOCK = 256
MOBA_TOPK = 3
ROPE_THETA = 500000.0
ROPE_DIM = 32
ROPE_HALF = ROPE_DIM // 2
ATTN_W = MOBA_HEADS * MOBA_HEAD_DIM
GMLP_DIM = 2048
GMLP_GROUPS = 16
GMLP_GROUP_DIM = 128
GMLP_CHUNK = 128

V7X_LANES = 128
V7X_SUBLANES = 8
V7X_VMEM_LIMIT_BYTES = 60 * 1024 * 1024
MIB = 1024 * 1024


def _cparams(n_grid, vmem_bytes):
    return pltpu.CompilerParams(
        dimension_semantics=("arbitrary",) * n_grid,
        vmem_limit_bytes=int(min(max(vmem_bytes, 16 * MIB), V7X_VMEM_LIMIT_BYTES)),
    )


def _split3(x):
    hi = x.astype(bf16)
    r1 = x - hi.astype(f32)
    mid = r1.astype(bf16)
    lo = (r1 - mid.astype(f32)).astype(bf16)
    return hi, mid, lo


def _dot(a, b):
    return jnp.dot(a, b, preferred_element_type=f32)


def _dot_nt(a, b):
    return lax.dot_general(a, b, (((1,), (1,)), ((), ())), preferred_element_type=f32)


def _dot_tn(a, b):
    return lax.dot_general(a, b, (((0,), (0,)), ((), ())), preferred_element_type=f32)


def _gelu_exact(x):
    return 0.5 * x * (1.0 + lax.erf(x * (1.0 / math.sqrt(2.0))))


def _mm_body(*refs, n_w, has_gain, has_res, use_scratch, epi):
    it = iter(refs)
    x_ref = next(it)
    g_ref = next(it) if has_gain else None
    w_refs = [next(it) for _ in range(n_w)]
    r_ref = next(it) if has_res else None
    o_ref = next(it)
    xs_ref = next(it) if use_scratch else None

    if use_scratch:
        @pl.when(pl.program_id(1) == 0)
        def _():
            x = x_ref[...].astype(f32)
            if has_gain:
                x = x * lax.rsqrt(jnp.mean(x * x, axis=-1, keepdims=True) + NORM_EPS) * g_ref[...]
            xs_ref[...] = x.astype(bf16)
        xb = xs_ref[...]
    else:
        xb = x_ref[...]
    accs = [_dot(xb, w[...].astype(bf16)) for w in w_refs]
    if epi == "swiglu":
        out = jax.nn.silu(accs[0]) * accs[1]
    elif epi == "gelu":
        out = _gelu_exact(accs[0])
    else:
        out = accs[0]
    if has_res:
        out = r_ref[...] + out
    o_ref[...] = out.astype(o_ref.dtype)


def _mm(x, ws, *, layer=None, n_out=None, gain=None, res=None, epi="none", out_dtype=f32, tn=512):
    m, k = x.shape
    n = n_out if n_out is not None else ws[0].shape[-1]
    tm = next((t for t in (1024, 512, 256, 128, 64, 32, 16, 8) if m % t == 0), m)
    tn = min(tn, n)
    assert m % tm == 0 and n % tn == 0
    n_w = len(ws)
    has_gain = gain is not None
    has_res = res is not None
    use_scratch = has_gain or x.dtype != bf16
    in_specs = [pl.BlockSpec((tm, k), lambda i, j: (i, 0))]
    args = [x]
    if has_gain:
        in_specs.append(pl.BlockSpec((1, k), lambda i, j: (0, 0)))
        args.append(gain.reshape(1, k).astype(f32))
    for w in ws:
        if layer is None:
            in_specs.append(pl.BlockSpec((k, tn), lambda i, j: (0, j)))
        else:
            in_specs.append(pl.BlockSpec((None, k, tn), lambda i, j: (layer, 0, j)))
        args.append(w)
    if has_res:
        in_specs.append(pl.BlockSpec((tm, tn), lambda i, j: (i, j)))
        args.append(res)
    scratch = [pltpu.VMEM((tm, k), bf16)] if use_scratch else []
    xbytes = x.dtype.itemsize
    est = (2 * tm * k * xbytes + (tm * k * 2 if use_scratch else 0) + n_w * k * tn * (2 * 4 + 2)
           + tm * tn * (2 * jnp.dtype(out_dtype).itemsize + (8 if has_res else 0) + 4 * n_w + 4) + 4 * MIB)
    body = functools.partial(_mm_body, n_w=n_w, has_gain=has_gain, has_res=has_res, use_scratch=use_scratch, epi=epi)
    return pl.pallas_call(
        body,
        grid=(m // tm, n // tn),
        in_specs=in_specs,
        out_specs=pl.BlockSpec((tm, tn), lambda i, j: (i, j)),
        out_shape=jax.ShapeDtypeStruct((m, n), out_dtype),
        scratch_shapes=scratch,
        compiler_params=_cparams(2, est),
        name="mm_" + epi,
    )(*args)


def _rms_body(x_ref, g_ref, o_ref):
    x = x_ref[...]
    o_ref[...] = x * lax.rsqrt(jnp.mean(x * x, axis=-1, keepdims=True) + NORM_EPS) * g_ref[...]


def _rms_norm(x, gain):
    m, d = x.shape
    tm = min(m, 512)
    return pl.pallas_call(
        _rms_body,
        grid=(m // tm,),
        in_specs=[pl.BlockSpec((tm, d), lambda i: (i, 0)), pl.BlockSpec((1, d), lambda i: (0, 0))],
        out_specs=pl.BlockSpec((tm, d), lambda i: (i, 0)),
        out_shape=jax.ShapeDtypeStruct((m, d), f32),
        compiler_params=_cparams(1, 4 * tm * d * 4 + 4 * MIB),
        name="final_rms",
    )(x, gain.reshape(1, d))


def _ssd_body(z_ref, xp_ref, bc_ref, dtq_ref, dtt_ref, cw_ref, cb_ref, dtb_ref, dtbt_ref, alog_ref, alogt_ref,
              dsk_ref, gn_ref, h0_ref, c0_ref, yg_ref, ssm_ref, ht_ref, xs_ref, xc_ref, *, valid_len):
    q = SSD_CHUNK
    c = pl.program_id(1)
    n_c = pl.num_programs(1)

    @pl.when(c == 0)
    def _():
        ht_ref[...] = h0_ref[0]
        xs_ref[0:V7X_SUBLANES, :] = c0_ref[0]

    xs_ref[V7X_SUBLANES:V7X_SUBLANES + q, 0:SSD_INNER] = xp_ref[0]
    xs_ref[V7X_SUBLANES:V7X_SUBLANES + q, SSD_INNER:SSD_CONV_DIM] = bc_ref[0]
    lane_step = 512
    for s in range(SSD_CONV_DIM // lane_step):
        sl = slice(s * lane_step, (s + 1) * lane_step)
        acc = cb_ref[:, sl] + cw_ref[SSD_CONV - 1:SSD_CONV, sl] * xs_ref[V7X_SUBLANES:V7X_SUBLANES + q, sl]
        for kk in range(1, SSD_CONV):
            acc = acc + cw_ref[SSD_CONV - 1 - kk:SSD_CONV - kk, sl] * xs_ref[V7X_SUBLANES - kk:V7X_SUBLANES - kk + q, sl]
        xc_ref[:, sl] = jax.nn.silu(acc)
    xs_ref[0:V7X_SUBLANES, :] = xs_ref[q:q + V7X_SUBLANES, :]

    ii = lax.broadcasted_iota(i32, (q, q), 0)
    jj = lax.broadcasted_iota(i32, (q, q), 1)
    causal = ii >= jj
    tril = jnp.where(causal, 1.0, 0.0).astype(bf16)
    triu = jnp.where(jj >= ii, 1.0, 0.0).astype(bf16)

    n_terms = 3
    hrow = lax.broadcasted_iota(i32, (n_terms * SSD_HPG, SSD_GROUP_W), 0) % SSD_HPG
    spread_x = jnp.where(lax.broadcasted_iota(i32, hrow.shape, 1) // SSD_HEAD_DIM == hrow, 1.0, 0.0).astype(bf16)
    prow = lax.broadcasted_iota(i32, (n_terms * SSD_HPG, SSD_HPG * q), 0) % SSD_HPG
    spread_q = jnp.where(lax.broadcasted_iota(i32, prow.shape, 1) // q == prow, 1.0, 0.0).astype(bf16)
    low_half = lax.broadcasted_iota(i32, (q, V7X_LANES), 1) < SSD_HEAD_DIM

    def spread(cols, mat):
        return _dot(jnp.concatenate(_split3(cols), axis=1), mat)

    def group(g):
        dt = jax.nn.softplus(dtq_ref[0, g] + dtb_ref[g])
        dtt = jax.nn.softplus(dtt_ref[0, g] + dtbt_ref[g])
        if valid_len is not None:
            dt = jnp.where(lax.broadcasted_iota(i32, dt.shape, 0) < valid_len, dt, 0.0)
            dtt = jnp.where(lax.broadcasted_iota(i32, dtt.shape, 1) < valid_len, dtt, 0.0)
        adt = dt * (-jnp.exp(alog_ref[g]))
        adtt = dtt * (-jnp.exp(alogt_ref[g]))
        acum = sum(_dot(tril, p) for p in _split3(adt))
        acumt = sum(_dot(p, triu) for p in _split3(adtt))
        last = acum[q - 1:q, :]
        col_q = spread(acum, spread_q)
        e_x = spread(jnp.exp(acum), spread_x)
        w_x = spread(jnp.exp(last - acum) * dt, spread_x)

        xoff = g * SSD_GROUP_W
        boff = SSD_INNER + g * SSD_STATE
        coff = SSD_INNER + SSD_GROUPS * SSD_STATE + g * SSD_STATE
        xg = xc_ref[:, pl.ds(xoff, SSD_GROUP_W)]
        bg = xc_ref[:, pl.ds(boff, SSD_STATE)].astype(bf16)
        cg = xc_ref[:, pl.ds(coff, SSD_STATE)].astype(bf16)
        cbm = _dot_nt(cg, bg)
        htg = ht_ref[:, pl.ds(xoff, SSD_GROUP_W)]
        yoff = _dot(cg, htg.astype(bf16))

        yds = []
        for pr in range(SSD_HPG // 2):
            xp = xg[:, pr * V7X_LANES:(pr + 1) * V7X_LANES]
            halves = (jnp.where(low_half, xp, 0.0).astype(bf16), jnp.where(low_half, 0.0, xp).astype(bf16))
            acc = None
            for r, xh in zip((2 * pr, 2 * pr + 1), halves):
                seg = col_q[:, r * q:(r + 1) * q] - acumt[r:r + 1, :]
                dec = jnp.exp(jnp.where(causal, seg, -jnp.inf))
                part = _dot((cbm * dec * dtt[r:r + 1, :]).astype(bf16), xh)
                acc = part if acc is None else acc + part
            yds.append(acc)
        yd = jnp.concatenate(yds, axis=1)
        yg = yd + yoff * e_x + dsk_ref[:, pl.ds(xoff, SSD_GROUP_W)] * xg

        ht_ref[:, pl.ds(xoff, SSD_GROUP_W)] = htg * e_x[q - 1:q, :] + _dot_tn(bg, (xg * w_x).astype(bf16))

        gz = yg * jax.nn.silu(z_ref[0, :, pl.ds(xoff, SSD_GROUP_W)])
        gzn = gz * lax.rsqrt(jnp.mean(gz * gz, axis=-1, keepdims=True) + NORM_EPS)
        yg_ref[0, :, pl.ds(xoff, SSD_GROUP_W)] = (gzn * gn_ref[:, pl.ds(xoff, SSD_GROUP_W)]).astype(yg_ref.dtype)

    for g in range(SSD_GROUPS):
        group(g)

    @pl.when(c == n_c - 1)
    def _():
        for g in range(SSD_GROUPS):
            sl = slice(g * SSD_GROUP_W, (g + 1) * SSD_GROUP_W)
            ssm_ref[0, sl, :] = ht_ref[:, sl].T


def _ssd_scan(proj, dt_raw, conv0, ssm0, conv_w, conv_b, dt_bias, a_log, d_skip, gate_norm, *, valid_len=None):
    b_, l_, _ = proj.shape
    q = SSD_CHUNK
    assert l_ % q == 0
    n_c = l_ // q
    g_, r_ = SSD_GROUPS, SSD_HPG
    dtq = dt_raw.reshape(b_, l_, g_, r_).transpose(0, 2, 1, 3)
    dtt = dt_raw.reshape(b_, l_, g_, r_).transpose(0, 2, 3, 1)
    h0t = ssm0.reshape(b_, SSD_INNER, SSD_STATE).transpose(0, 2, 1)
    c0 = jnp.pad(conv0, ((0, 0), (V7X_SUBLANES - (SSD_CONV - 1), 0), (0, 0)))
    dtb = dt_bias.reshape(g_, 1, r_)
    dtbt = dt_bias.reshape(g_, r_, 1)
    alog = a_log.reshape(g_, 1, r_)
    alogt = a_log.reshape(g_, r_, 1)
    dsk = jnp.repeat(d_skip, SSD_HEAD_DIM).reshape(1, SSD_INNER)
    full = lambda shape: pl.BlockSpec(shape, lambda b, c: (0,) * len(shape))
    in_specs = [
        pl.BlockSpec((1, q, SSD_INNER), lambda b, c: (b, c, 0)),
        pl.BlockSpec((1, q, SSD_INNER), lambda b, c: (b, c, 1)),
        pl.BlockSpec((1, q, SSD_BC_W), lambda b, c: (b, c, 2 * SSD_INNER // SSD_BC_W)),
        pl.BlockSpec((1, g_, q, r_), lambda b, c: (b, 0, c, 0)),
        pl.BlockSpec((1, g_, r_, q), lambda b, c: (b, 0, 0, c)),
        full((SSD_CONV, SSD_CONV_DIM)), full((1, SSD_CONV_DIM)),
        full((g_, 1, r_)), full((g_, r_, 1)), full((g_, 1, r_)), full((g_, r_, 1)), full((1, SSD_INNER)),
        full((1, SSD_INNER)),
        pl.BlockSpec((1, SSD_STATE, SSD_INNER), lambda b, c: (b, 0, 0)),
        pl.BlockSpec((1, V7X_SUBLANES, SSD_CONV_DIM), lambda b, c: (b, 0, 0)),
    ]
    out_specs = [
        pl.BlockSpec((1, q, SSD_INNER), lambda b, c: (b, c, 0)),
        pl.BlockSpec((1, SSD_INNER, SSD_STATE), lambda b, c: (b, 0, 0)),
    ]
    yg, ssm = pl.pallas_call(
        functools.partial(_ssd_body, valid_len=valid_len),
        grid=(b_, n_c),
        in_specs=in_specs,
        out_specs=out_specs,
        out_shape=[jax.ShapeDtypeStruct((b_, l_, SSD_INNER), bf16),
                   jax.ShapeDtypeStruct((b_, SSD_INNER, SSD_STATE), f32)],
        scratch_shapes=[pltpu.VMEM((SSD_STATE, SSD_INNER), f32),
                        pltpu.VMEM((q + 2 * V7X_SUBLANES, SSD_CONV_DIM), f32),
                        pltpu.VMEM((q, SSD_CONV_DIM), f32)],
        compiler_params=_cparams(2, 48 * MIB),
        name="ssd_scan",
    )(proj, proj, proj, dtq, dtt, conv_w, conv_b.reshape(1, -1), dtb, dtbt, alog, alogt, dsk,
      gate_norm.reshape(1, -1), h0t, c0)
    return yg, ssm.reshape(b_, SSD_HEADS, SSD_HEAD_DIM, SSD_STATE)


def _trig_body(pos_ref, invf_ref, sgn_ref, cos_ref, sin_ref):
    ang = pos_ref[...].astype(f32) * invf_ref[...]
    rot = lax.broadcasted_iota(i32, ang.shape, 1) < ROPE_DIM
    cos_ref[...] = jnp.where(rot, jnp.cos(ang), 1.0)
    sin_ref[...] = jnp.sin(ang) * sgn_ref[...]


def _trig_tables(pos):
    n = pos.shape[0]
    inv_freq = ROPE_THETA ** (-jnp.arange(ROPE_HALF, dtype=f32) * 2.0 / ROPE_DIM)
    pad = jnp.zeros((MOBA_HEAD_DIM - ROPE_DIM,), f32)
    invf = jnp.concatenate([inv_freq, inv_freq, pad]).reshape(1, MOBA_HEAD_DIM)
    sgn = jnp.concatenate([-jnp.ones((ROPE_HALF,), f32), jnp.ones((ROPE_HALF,), f32), pad]).reshape(1, MOBA_HEAD_DIM)
    out = jax.ShapeDtypeStruct((n, MOBA_HEAD_DIM), f32)
    return pl.pallas_call(_trig_body, out_shape=[out, out], name="rope_tables")(pos.reshape(n, 1).astype(i32), invf, sgn)


def _rotate(x, cos_t, sin_t):
    lane = lax.broadcasted_iota(i32, x.shape, 1)
    partner = jnp.where(lane < ROPE_HALF, pltpu.roll(x, MOBA_HEAD_DIM - ROPE_HALF, 1), pltpu.roll(x, ROPE_HALF, 1))
    return x * cos_t + partner * sin_t


def _prep_body(qkv_ref, cos_ref, sin_ref, q_ref, k_ref, kb_ref, vb_ref, *maybe_kmean, with_kmean):
    cos_t = cos_ref[...]
    sin_t = sin_ref[...]
    for h in range(MOBA_HEADS):
        sl = slice(h * MOBA_HEAD_DIM, (h + 1) * MOBA_HEAD_DIM)
        q_ref[:, sl] = _rotate(qkv_ref[:, sl], cos_t, sin_t)
        kr = _rotate(qkv_ref[:, ATTN_W + h * MOBA_HEAD_DIM:ATTN_W + (h + 1) * MOBA_HEAD_DIM], cos_t, sin_t)
        k_ref[:, sl] = kr
        kb_ref[:, sl] = kr.astype(bf16)
        if with_kmean:
            maybe_kmean[0][0, :, sl] = jnp.sum(kr, axis=0, keepdims=True) * (1.0 / MOBA_BLOCK)
    vb_ref[...] = qkv_ref[:, 2 * ATTN_W:3 * ATTN_W].astype(bf16)


def _moba_prep(qkv, cos_t, sin_t, *, rows, with_kmean):
    m = qkv.shape[0]
    n_tab = cos_t.shape[0] // rows
    row_spec = lambda w: pl.BlockSpec((rows, w), lambda i: (i, 0))
    tab_spec = pl.BlockSpec((rows, MOBA_HEAD_DIM), lambda i: (i % n_tab, 0))
    out_shape = [jax.ShapeDtypeStruct((m, ATTN_W), f32), jax.ShapeDtypeStruct((m, ATTN_W), f32),
                 jax.ShapeDtypeStruct((m, ATTN_W), bf16), jax.ShapeDtypeStruct((m, ATTN_W), bf16)]
    out_specs = [row_spec(ATTN_W)] * 4
    if with_kmean:
        out_shape.append(jax.ShapeDtypeStruct((m // rows, 1, ATTN_W), f32))
        out_specs.append(pl.BlockSpec((1, 1, ATTN_W), lambda i: (i, 0, 0)))
    return pl.pallas_call(
        functools.partial(_prep_body, with_kmean=with_kmean),
        grid=(m // rows,),
        in_specs=[row_spec(3 * ATTN_W), tab_spec, tab_spec],
        out_specs=out_specs,
        out_shape=out_shape,
        compiler_params=_cparams(1, 40 * MIB),
        name="moba_prep",
    )(qkv, cos_t, sin_t)


def _gate_scores(q, km):
    qh, qm, _ = _split3(q)
    kh, kmid, _ = _split3(km)
    return _dot_nt(qh, kh) + _dot_nt(qh, kmid) + _dot_nt(qm, kh)


def _top_blocks(gate, n_take):
    lane = lax.broadcasted_iota(i32, gate.shape, 1)
    picks = []
    for _ in range(n_take):
        mx = jnp.max(gate, axis=-1, keepdims=True)
        idx = jnp.min(jnp.where(gate == mx, lane, gate.shape[1]), axis=-1, keepdims=True)
        picks.append(idx)
        gate = jnp.where(lane == idx, -jnp.inf, gate)
    return picks


ATTN_HEADS_PER_STEP = 2


def _attn_prompt_body(q_ref, kb_ref, vb_ref, km_ref, o_ref, s_ref):
    blk = MOBA_BLOCK
    hd = MOBA_HEAD_DIM
    cur = pl.program_id(2)
    n_blk = km_ref.shape[1]
    n_all = n_blk * blk
    scale = hd ** -0.5
    blk_id = lax.broadcasted_iota(i32, (n_blk, blk), 0)
    q_pos = lax.broadcasted_iota(i32, (n_blk, blk), 1)
    past = blk_id < cur
    col = lax.broadcasted_iota(i32, (blk, blk), 1).astype(f32)
    spread = jnp.where(lax.broadcasted_iota(i32, (n_blk, n_all), 1) // blk == lax.broadcasted_iota(i32, (n_blk, n_all), 0),
                       1.0, 0.0).astype(bf16)
    ones = jnp.ones((blk, hd), bf16)

    for hh in range(ATTN_HEADS_PER_STEP):
        hs = slice(hh * hd, (hh + 1) * hd)
        q = q_ref[:, hs]
        gate = jnp.where(past, _gate_scores(km_ref[0, :, hs], q), -jnp.inf)
        sel = jnp.zeros((n_blk, blk), i32)
        for _ in range(min(MOBA_TOPK, n_blk)):
            mx = jnp.max(gate, axis=0, keepdims=True)
            idx = jnp.min(jnp.where(gate == mx, blk_id, n_blk), axis=0, keepdims=True)
            sel = jnp.where((blk_id == idx) & past, 1, sel)
            gate = jnp.where(blk_id == idx, -jnp.inf, gate)
        last_t = jnp.where(blk_id == cur, q_pos, jnp.where(sel > 0, blk - 1, -1)).astype(f32).astype(bf16)
        qs = (q * scale).astype(bf16)

        def attend(n_keys, qs=qs, last_t=last_t, hs=hs):
            last = _dot_tn(last_t, spread[:, :n_keys])
            m2 = None
            for n in range(n_keys // blk):
                keys = slice(n * blk, (n + 1) * blk)
                s = jnp.where(col <= last[:, keys], _dot_nt(qs, kb_ref[keys, hs]), -jnp.inf)
                s_ref[:, keys] = s
                mx = jnp.maximum(s[:, :hd], s[:, hd:])
                m2 = mx if m2 is None else jnp.maximum(m2, mx)
            m = jnp.broadcast_to(jnp.max(m2, axis=-1, keepdims=True), (blk, blk))
            acc = None
            for n in range(n_keys // blk):
                keys = slice(n * blk, (n + 1) * blk)
                p = jnp.exp(s_ref[:, keys] - m).astype(bf16)
                pv = _dot(p, jnp.concatenate([vb_ref[keys, hs], ones], axis=1))
                acc = pv if acc is None else acc + pv
            o_ref[:, hs] = (acc[:, :hd] / acc[:, hd:]).astype(o_ref.dtype)

        if n_blk % 2 == 0 and n_blk >= 4:
            half = n_blk // 2
            pl.when(cur < half)(functools.partial(attend, half * blk))
            pl.when(cur >= half)(functools.partial(attend, n_blk * blk))
        else:
            attend(n_blk * blk)


def _moba_attn_prompt(q, kb, vb, kmean, b_, l_):
    n_blk = l_ // MOBA_BLOCK
    w = ATTN_HEADS_PER_STEP * MOBA_HEAD_DIM
    return pl.pallas_call(
        _attn_prompt_body,
        grid=(b_, MOBA_HEADS // ATTN_HEADS_PER_STEP, n_blk),
        in_specs=[
            pl.BlockSpec((MOBA_BLOCK, w), lambda b, h, t: (b * n_blk + t, h)),
            pl.BlockSpec((l_, w), lambda b, h, t: (b, h)),
            pl.BlockSpec((l_, w), lambda b, h, t: (b, h)),
            pl.BlockSpec((1, n_blk, w), lambda b, h, t: (b, 0, h)),
        ],
        out_specs=pl.BlockSpec((MOBA_BLOCK, w), lambda b, h, t: (b * n_blk + t, h)),
        out_shape=jax.ShapeDtypeStruct((b_ * l_, ATTN_W), bf16),
        scratch_shapes=[pltpu.VMEM((MOBA_BLOCK, l_), f32)],
        compiler_params=_cparams(3, 32 * MIB),
        name="moba_attn_prompt",
    )(q, kb, vb, kmean)


PAGES_PER_BLOCK = MOBA_BLOCK // PAGE_SIZE
KMEAN_BLOCKS_PER_STEP = 8
KMEAN_PAGES_PER_STEP = KMEAN_BLOCKS_PER_STEP * PAGES_PER_BLOCK


def _kmean_body(pt_ref, *refs):
    page_refs, o_ref = refs[:-1], refs[-1]
    for blk in range(KMEAN_BLOCKS_PER_STEP):
        tot = None
        for p in range(PAGES_PER_BLOCK):
            part = jnp.sum(page_refs[blk * PAGES_PER_BLOCK + p][...], axis=0)
            tot = part if tot is None else tot + part
        tot = tot * (1.0 / MOBA_BLOCK)
        for h in range(MOBA_HEADS):
            o_ref[0, blk:blk + 1, h * MOBA_HEAD_DIM:(h + 1) * MOBA_HEAD_DIM] = tot[h:h + 1, :]


def _cache_kmean(cache, layer, page_table):
    b_, n_pages = page_table.shape
    n_steps = n_pages // KMEAN_PAGES_PER_STEP
    pt_flat = page_table.reshape(-1)

    def page_spec(p):
        return pl.BlockSpec((None, None, PAGE_SIZE, MOBA_HEADS, MOBA_HEAD_DIM),
                            lambda b, s, pt: (layer, pt[b * n_pages + s * KMEAN_PAGES_PER_STEP + p], 0, 0, 0))

    return pl.pallas_call(
        _kmean_body,
        grid_spec=pltpu.PrefetchScalarGridSpec(
            num_scalar_prefetch=1,
            grid=(b_, n_steps),
            in_specs=[page_spec(p) for p in range(KMEAN_PAGES_PER_STEP)],
            out_specs=pl.BlockSpec((1, KMEAN_BLOCKS_PER_STEP, ATTN_W), lambda b, s, pt: (b, s, 0)),
        ),
        out_shape=jax.ShapeDtypeStruct((b_, n_pages // PAGES_PER_BLOCK, ATTN_W), f32),
        compiler_params=_cparams(2, 40 * MIB),
        name="cache_kmean",
    )(pt_flat, *([cache] * KMEAN_PAGES_PER_STEP))


def _sample_topk_body(q_ref, km_ref, o_ref, g_ref):
    rows = q_ref.shape[1]
    for h in range(MOBA_HEADS):
        sl = slice(h * MOBA_HEAD_DIM, (h + 1) * MOBA_HEAD_DIM)
        g_ref[h * rows:(h + 1) * rows, :] = _gate_scores(q_ref[0, :, sl], km_ref[0, :, sl])
    picks = _top_blocks(g_ref[...], MOBA_TOPK)
    lane = lax.broadcasted_iota(i32, o_ref.shape[1:], 1)
    out = jnp.zeros(o_ref.shape[1:], i32)
    for s, idx in enumerate(picks):
        out = jnp.where(lane == s, idx, out)
    o_ref[0] = out


def _sample_topk(q8, kmean):
    b_, rows, _ = q8.shape
    n_blk = kmean.shape[1]
    return pl.pallas_call(
        _sample_topk_body,
        grid=(b_,),
        in_specs=[pl.BlockSpec((1, rows, ATTN_W), lambda b: (b, 0, 0)),
                  pl.BlockSpec((1, n_blk, ATTN_W), lambda b: (b, 0, 0))],
        out_specs=pl.BlockSpec((1, MOBA_HEADS * rows, V7X_LANES), lambda b: (b, 0, 0)),
        out_shape=jax.ShapeDtypeStruct((b_, MOBA_HEADS * rows, V7X_LANES), i32),
        scratch_shapes=[pltpu.VMEM((MOBA_HEADS * rows, n_blk), f32)],
        compiler_params=_cparams(1, 16 * MIB),
        name="sample_topk",
    )(q8, kmean)


def _attn_sample_body(sel_ref, pt_ref, q_ref, kn_ref, vn_ref, ck_hbm, cv_hbm, o_ref, kbuf, vbuf, sems,
                      *, layer, n_tok, n_pages):
    n_sel = n_tok * MOBA_TOPK * PAGES_PER_BLOCK
    n_heads = pl.num_programs(1)
    step = pl.program_id(0) * n_heads + pl.program_id(1)
    n_steps = pl.num_programs(0) * n_heads
    slot = lax.rem(step, 2)

    def page_copies(step_, slot_):
        b_ = step_ // n_heads
        h_ = lax.rem(step_, n_heads)
        out = []
        for i in range(n_sel):
            ts, p = divmod(i, PAGES_PER_BLOCK)
            blk = sel_ref[step_ * (n_tok * MOBA_TOPK) + ts]
            page = pt_ref[b_ * n_pages + blk * PAGES_PER_BLOCK + p]
            out.append(pltpu.make_async_copy(ck_hbm.at[layer, page, :, h_, :], kbuf.at[slot_, i], sems.at[0, slot_]))
            out.append(pltpu.make_async_copy(cv_hbm.at[layer, page, :, h_, :], vbuf.at[slot_, i], sems.at[1, slot_]))
        return out

    @pl.when(step == 0)
    def _():
        for cp in page_copies(step, slot):
            cp.start()

    @pl.when(step + 1 < n_steps)
    def _():
        for cp in page_copies(step + 1, 1 - slot):
            cp.start()

    for cp in page_copies(step, slot):
        cp.wait()

    rows = q_ref.shape[1]
    scale = MOBA_HEAD_DIM ** -0.5
    qb = q_ref[0].astype(bf16)
    row = lax.broadcasted_iota(i32, (rows, PAGE_SIZE), 0)
    k_refs = [kbuf.at[slot, i] for i in range(n_sel)]
    v_refs = [vbuf.at[slot, i] for i in range(n_sel)]
    scores = []
    for i in range(n_sel):
        t = i // (MOBA_TOPK * PAGES_PER_BLOCK)
        s = _dot_nt(qb, k_refs[i][...].astype(bf16)) * scale
        scores.append(jnp.where(row == t, s, -jnp.inf))
    r8 = lax.broadcasted_iota(i32, (rows, rows), 0)
    c8 = lax.broadcasted_iota(i32, (rows, rows), 1)
    s_own = _dot_nt(qb, kn_ref[0]) * scale
    s_own = jnp.where((c8 <= r8) & (c8 < n_tok), s_own, -jnp.inf)
    m = jnp.max(s_own, axis=-1, keepdims=True)
    for s in scores:
        m = jnp.maximum(m, jnp.max(s, axis=-1, keepdims=True))
    m = jnp.where(m == -jnp.inf, 0.0, m)
    p_own = jnp.exp(s_own - m)
    l = jnp.sum(p_own, axis=-1, keepdims=True)
    acc = _dot(p_own.astype(bf16), vn_ref[0])
    for i in range(n_sel):
        p = jnp.exp(scores[i] - m)
        l = l + jnp.sum(p, axis=-1, keepdims=True)
        acc = acc + _dot(p.astype(bf16), v_refs[i][...].astype(bf16))
    l = jnp.where(l == 0.0, 1.0, l)
    o_ref[0] = (acc / l).astype(o_ref.dtype)


def _moba_attn_sample(sel, page_table, q8, kn8, vn8, cache_k, cache_v, layer, n_tok):
    b_, rows, _ = q8.shape
    n_pages = page_table.shape[1]
    sel_flat = sel.reshape(-1).astype(i32)
    pt_flat = page_table.reshape(-1).astype(i32)
    head_spec = pl.BlockSpec((1, rows, MOBA_HEAD_DIM), lambda b, h, sel_r, pt_r: (b, 0, h))
    n_sel = n_tok * MOBA_TOPK * PAGES_PER_BLOCK
    page_buf = pltpu.VMEM((2, n_sel, PAGE_SIZE, MOBA_HEAD_DIM), f32)
    return pl.pallas_call(
        functools.partial(_attn_sample_body, layer=layer, n_tok=n_tok, n_pages=n_pages),
        grid_spec=pltpu.PrefetchScalarGridSpec(
            num_scalar_prefetch=2,
            grid=(b_, MOBA_HEADS),
            in_specs=[head_spec, head_spec, head_spec,
                      pl.BlockSpec(memory_space=pl.ANY), pl.BlockSpec(memory_space=pl.ANY)],
            out_specs=head_spec,
            scratch_shapes=[page_buf, page_buf, pltpu.SemaphoreType.DMA((2, 2))],
        ),
        out_shape=jax.ShapeDtypeStruct((b_, rows, ATTN_W), bf16),
        compiler_params=_cparams(2, 24 * MIB),
        name="moba_attn_sample",
    )(sel_flat, pt_flat, q8, kn8, vn8, cache_k, cache_v)


def _gmlp_body(uv_ref, lg_ref, lb_ref, ws_ref, bst_ref, o_ref, *maybe_v, with_v):
    n = GMLP_CHUNK
    vf = uv_ref[:, GMLP_DIM:2 * GMLP_DIM]
    mu = jnp.mean(vf, axis=-1, keepdims=True)
    var = jnp.mean(jnp.square(vf - mu), axis=-1, keepdims=True)
    v = (vf - mu) * lax.rsqrt(var + NORM_EPS) * lg_ref[...] + lb_ref[...]
    if with_v:
        maybe_v[0][...] = v
    ii = lax.broadcasted_iota(i32, (n, n), 0)
    jj = lax.broadcasted_iota(i32, (n, n), 1)
    for g in range(GMLP_GROUPS):
        sl = slice(g * GMLP_GROUP_DIM, (g + 1) * GMLP_GROUP_DIM)
        w = jnp.where(jj <= ii, ws_ref[g], 0.0).astype(bf16)
        s = _dot(w, v[:, sl].astype(bf16)) + bst_ref[:, g:g + 1]
        o_ref[:, sl] = (uv_ref[:, sl] * s).astype(o_ref.dtype)


def _gmlp_gate(uv, ln_g, ln_b, w_s, b_s, *, with_v):
    m = uv.shape[0]
    n = GMLP_CHUNK
    out_shape = [jax.ShapeDtypeStruct((m, GMLP_DIM), bf16)]
    out_specs = [pl.BlockSpec((n, GMLP_DIM), lambda i: (i, 0))]
    if with_v:
        out_shape.append(jax.ShapeDtypeStruct((m, GMLP_DIM), f32))
        out_specs.append(pl.BlockSpec((n, GMLP_DIM), lambda i: (i, 0)))
    return pl.pallas_call(
        functools.partial(_gmlp_body, with_v=with_v),
        grid=(m // n,),
        in_specs=[pl.BlockSpec((n, 2 * GMLP_DIM), lambda i: (i, 0)),
                  pl.BlockSpec((1, GMLP_DIM), lambda i: (0, 0)),
                  pl.BlockSpec((1, GMLP_DIM), lambda i: (0, 0)),
                  pl.BlockSpec((GMLP_GROUPS, n, n), lambda i: (0, 0, 0)),
                  pl.BlockSpec((n, GMLP_GROUPS), lambda i: (0, 0))],
        out_specs=out_specs,
        out_shape=out_shape,
        compiler_params=_cparams(1, 24 * MIB),
        name="gmlp_gate",
    )(uv, ln_g.reshape(1, -1), ln_b.reshape(1, -1), w_s, b_s.T)


def _pad_rows(x, rows):
    return jnp.pad(x, ((0, 0), (0, rows - x.shape[1]), (0, 0)))


def _ssd_layer(h, b_, l_, gain, conv0, ssm0, j, w_in, w_dt, conv_w, conv_b, dt_bias, a_log, d_skip, gate_norm, w_out):
    proj = _mm(h, [w_in], layer=j, n_out=SSD_MAIN_W, gain=gain)
    dt_raw = _mm(h, [w_dt], gain=gain)
    proj3 = proj.reshape(b_, l_, SSD_MAIN_W)
    dt3 = dt_raw.reshape(b_, l_, SSD_HEADS)
    xbc_ext = jnp.concatenate([conv0, proj3[:, :, SSD_INNER:]], axis=1) if l_ < SSD_CONV - 1 else None
    if xbc_ext is not None:
        new_conv = xbc_ext[:, -(SSD_CONV - 1):]
    else:
        new_conv = proj3[:, l_ - (SSD_CONV - 1):, SSD_INNER:]
    lp = -(-l_ // SSD_CHUNK) * SSD_CHUNK
    valid_len = None
    if lp != l_:
        assert lp == SSD_CHUNK
        proj3 = _pad_rows(proj3, lp)
        dt3 = _pad_rows(dt3, lp)
        valid_len = l_
    yg, ssm = _ssd_scan(proj3, dt3, conv0, ssm0, conv_w, conv_b, dt_bias, a_log, d_skip, gate_norm, valid_len=valid_len)
    if lp != l_:
        yg = yg[:, :l_]
    h = _mm(yg.reshape(b_ * l_, SSD_INNER), [w_out], layer=j, res=h)
    return h, new_conv, ssm


def _ffn_layer(h, i, gain, w_gate, w_up, w_down):
    t = _mm(h, [w_gate, w_up], layer=i, gain=gain, epi="swiglu", out_dtype=bf16)
    return _mm(t, [w_down], layer=i, res=h, tn=256)


def kernel(x_prompt, x_sample, state_ssm, state_conv, cache_k, cache_v, page_table, norm_mix, norm_ffn, norm_final, ssd_w_in, ssd_conv_w, ssd_conv_b, ssd_dt_bias, ssd_a_log, ssd_d, ssd_gate_norm, ssd_w_out, moba_w_qkv, moba_w_o, gmlp_w_in, gmlp_ln_g, gmlp_ln_b, gmlp_w_s, gmlp_b_s, gmlp_w_out, ffn_w_gate, ffn_w_up, ffn_w_down):
    bp, lp_, d = x_prompt.shape
    bs, ls, _ = x_sample.shape
    hp = x_prompt.reshape(bp * lp_, d)
    hs = x_sample.reshape(bs * ls, d)
    n_pages = page_table.shape[1]
    past = n_pages * PAGE_SIZE
    tok_rows = V7X_SUBLANES
    assert ls <= tok_rows and past % MOBA_BLOCK == 0 and lp_ % MOBA_BLOCK == 0

    ssm_p, ssm_s, conv_p, conv_s = [], [], [], []
    k_p, v_p, k_s, v_s, gv_s = [], [], [], [], []
    for i in range(DEPTH):
        kind = i % N_MIXERS
        j = i // N_MIXERS
        if kind == 0:
            w = (j, ssd_w_in, ssd_w_in[j, :, SSD_MAIN_W:], ssd_conv_w[j], ssd_conv_b[j], ssd_dt_bias[j], ssd_a_log[j],
                 ssd_d[j], ssd_gate_norm[j], ssd_w_out)
            zero_conv = jnp.zeros((bp, SSD_CONV - 1, SSD_CONV_DIM), f32)
            zero_ssm = jnp.zeros((bp, SSD_HEADS, SSD_HEAD_DIM, SSD_STATE), f32)
            hp, cp_, sp_ = _ssd_layer(hp, bp, lp_, norm_mix[i], zero_conv, zero_ssm, *w)
            hs, cs_, ss_ = _ssd_layer(hs, bs, ls, norm_mix[i], state_conv[j], state_ssm[j], *w)
            conv_p.append(cp_)
            ssm_p.append(sp_)
            conv_s.append(cs_)
            ssm_s.append(ss_)
        elif kind == 1:
            qkv = _mm(hp, [moba_w_qkv], layer=j, gain=norm_mix[i])
            cos_t, sin_t = _trig_tables(jnp.arange(lp_, dtype=i32))
            q, k, kb, vb, kmean = _moba_prep(qkv, cos_t, sin_t, rows=MOBA_BLOCK, with_kmean=True)
            o = _moba_attn_prompt(q, kb, vb, kmean.reshape(bp, lp_ // MOBA_BLOCK, ATTN_W), bp, lp_)
            hp = _mm(o, [moba_w_o], layer=j, res=hp)
            k_p.append(k.reshape(bp, lp_, MOBA_HEADS, MOBA_HEAD_DIM))
            v_p.append(qkv[:, 2 * ATTN_W:].reshape(bp, lp_, MOBA_HEADS, MOBA_HEAD_DIM))
            qkv_s = _mm(hs, [moba_w_qkv], layer=j, gain=norm_mix[i])
            pos_s = past + (jnp.arange(bs * ls, dtype=i32) % ls)
            cos_s, sin_s = _trig_tables(pos_s)
            q_s, kn, knb, vnb = _moba_prep(qkv_s, cos_s, sin_s, rows=bs * ls, with_kmean=False)
            kmean_c = _cache_kmean(cache_k, j, page_table)
            q8 = _pad_rows(q_s.reshape(bs, ls, ATTN_W), tok_rows)
            kn8 = _pad_rows(knb.reshape(bs, ls, ATTN_W), tok_rows)
            vn8 = _pad_rows(vnb.reshape(bs, ls, ATTN_W), tok_rows)
            top = _sample_topk(q8, kmean_c)
            sel = top.reshape(bs, MOBA_HEADS, tok_rows, V7X_LANES)[:, :, :ls, :MOBA_TOPK]
            o_s = _moba_attn_sample(sel, page_table, q8, kn8, vn8, cache_k, cache_v, j, ls)
            hs = _mm(o_s[:, :ls].reshape(bs * ls, ATTN_W), [moba_w_o], layer=j, res=hs)
            k_s.append(kn.reshape(bs, ls, MOBA_HEADS, MOBA_HEAD_DIM))
            v_s.append(qkv_s[:, 2 * ATTN_W:].reshape(bs, ls, MOBA_HEADS, MOBA_HEAD_DIM))
        else:
            w = (gmlp_ln_g[j], gmlp_ln_b[j], gmlp_w_s[j], gmlp_b_s[j])
            uv = _mm(hp, [gmlp_w_in], layer=j, gain=norm_mix[i], epi="gelu")
            (gated,) = _gmlp_gate(uv, *w, with_v=False)
            hp = _mm(gated, [gmlp_w_out], layer=j, res=hp)
            uv_s = _mm(hs, [gmlp_w_in], layer=j, gain=norm_mix[i], epi="gelu")
            uv_s = _pad_rows(uv_s.reshape(bs, ls, 2 * GMLP_DIM), GMLP_CHUNK).reshape(bs * GMLP_CHUNK, 2 * GMLP_DIM)
            gated_s, v_rows = _gmlp_gate(uv_s, *w, with_v=True)
            gated_s = gated_s.reshape(bs, GMLP_CHUNK, GMLP_DIM)[:, :ls].reshape(bs * ls, GMLP_DIM)
            hs = _mm(gated_s, [gmlp_w_out], layer=j, res=hs)
            gv_s.append(v_rows.reshape(bs, GMLP_CHUNK, GMLP_DIM)[:, :ls])
        hp = _ffn_layer(hp, i, norm_ffn[i], ffn_w_gate, ffn_w_up, ffn_w_down)
        hs = _ffn_layer(hs, i, norm_ffn[i], ffn_w_gate, ffn_w_up, ffn_w_down)
    y_prompt = _rms_norm(hp, norm_final).reshape(bp, lp_, d)
    y_sample = _rms_norm(hs, norm_final).reshape(bs, ls, d)
    return (y_prompt, y_sample, jnp.stack(ssm_p), jnp.stack(ssm_s), jnp.stack(conv_p), jnp.stack(conv_s),
            jnp.stack(k_p), jnp.stack(v_p), jnp.stack(k_s), jnp.stack(v_s), jnp.stack(gv_s))
```

```python
import functools
import math

import jax
import jax.numpy as jnp
from jax import lax
from jax.experimental import pallas as pl
from jax.experimental.pallas import tpu as pltpu

f32 = jnp.float32
bf16 = jnp.bfloat16
i32 = jnp.int32

D_MODEL = 2048
DEPTH = 4
PAGE_SIZE = 128
N_MIXERS = 3
NORM_EPS = 1e-6
SSD_INNER = 4096
SSD_HEAD_DIM = 64
SSD_HEADS = 64
SSD_GROUPS = 8
SSD_HPG = 8
SSD_STATE = 128
SSD_CONV = 4
SSD_CHUNK = 128
SSD_GROUP_W = SSD_HPG * SSD_HEAD_DIM
SSD_BC_W = 2 * SSD_GROUPS * SSD_STATE
SSD_CONV_DIM = SSD_INNER + SSD_BC_W
SSD_MAIN_W = SSD_INNER + SSD_CONV_DIM
MOBA_HEADS = 16
MOBA_HEAD_DIM = 128
MOBA_BLOCK = 256
MOBA_TOPK = 3
ROPE_THETA = 500000.0
ROPE_DIM = 32
ROPE_HALF = ROPE_DIM // 2
ATTN_W = MOBA_HEADS * MOBA_HEAD_DIM
GMLP_DIM = 2048
GMLP_GROUPS = 16
GMLP_GROUP_DIM = 128
GMLP_CHUNK = 128

V7X_LANES = 128
V7X_SUBLANES = 8
V7X_VMEM_LIMIT_BYTES = 60 * 1024 * 1024
MIB = 1024 * 1024


def _cparams(n_grid, vmem_bytes):
    return pltpu.CompilerParams(
        dimension_semantics=("arbitrary",) * n_grid,
        vmem_limit_bytes=int(min(max(vmem_bytes, 16 * MIB), V7X_VMEM_LIMIT_BYTES)),
    )


def _split3(x):
    hi = x.astype(bf16)
    r1 = x - hi.astype(f32)
    mid = r1.astype(bf16)
    lo = (r1 - mid.astype(f32)).astype(bf16)
    return hi, mid, lo


def _dot(a, b):
    return jnp.dot(a, b, preferred_element_type=f32)


def _dot_nt(a, b):
    return lax.dot_general(a, b, (((1,), (1,)), ((), ())), preferred_element_type=f32)


def _dot_tn(a, b):
    return lax.dot_general(a, b, (((0,), (0,)), ((), ())), preferred_element_type=f32)


def _gelu_exact(x):
    return 0.5 * x * (1.0 + lax.erf(x * (1.0 / math.sqrt(2.0))))


MM_MAX_ROWS = 2048
MM_VMEM_BUDGET = 50 * MIB


def _mm_body(*refs, n_w, has_gain, has_res, epi, tm, ms, w_t):
    it = iter(refs)
    x_ref = next(it)
    sx_ref = next(it) if ms else None
    g_ref = next(it) if has_gain else None
    w_refs = [next(it) for _ in range(n_w)]
    r_ref = next(it) if has_res else None
    sr_ref = next(it) if (has_res and ms) else None
    o_ref = next(it)
    so_ref = next(it) if ms else None
    xs_ref = next(it)
    i = pl.program_id(0)
    j = pl.program_id(1)

    def prep(v):
        v = v.astype(f32)
        if has_gain:
            v = v * lax.rsqrt(jnp.mean(v * v, axis=-1, keepdims=True) + NORM_EPS) * g_ref[...]
        return v.astype(bf16)

    @pl.when(j == 0)
    def _():
        xs_ref[0:tm, :] = prep(x_ref[...])

    if ms:
        @pl.when((j == 0) & (i == 0))
        def _():
            xs_ref[tm:tm + ms, :] = prep(sx_ref[...])

    wb = [w[...].astype(bf16) for w in w_refs]

    def run(rows):
        xb = xs_ref[0:rows, :]
        accs = [(_dot_nt if w_t else _dot)(xb, w) for w in wb]
        if epi == "swiglu":
            out = jax.nn.silu(accs[0]) * accs[1]
        elif epi == "gelu":
            out = _gelu_exact(accs[0])
        else:
            out = accs[0]
        main = out[0:tm]
        if has_res:
            main = r_ref[...] + main
        o_ref[...] = main.astype(o_ref.dtype)
        if rows > tm:
            side = out[tm:rows]
            if has_res:
                side = sr_ref[...] + side
            so_ref[...] = side.astype(so_ref.dtype)

    if ms:
        pl.when(i == 0)(functools.partial(run, tm + ms))

        @pl.when(i > 0)
        def _():
            run(tm)
            so_ref[...] = jnp.zeros(so_ref.shape, so_ref.dtype)
    else:
        run(tm)


def _mm(x, ws, *, layer=None, w_t=False, col0=0, n_out=None, gain=None, res=None, side_x=None, side_res=None,
        epi="none", out_dtype=f32):
    m, k = x.shape
    n_total = ws[0].shape[-2] if w_t else ws[0].shape[-1]
    n = n_out if n_out is not None else n_total - col0
    ms = 0 if side_x is None else side_x.shape[0]
    n_w = len(ws)
    has_gain = gain is not None
    has_res = res is not None
    out_b = jnp.dtype(out_dtype).itemsize

    def vmem_estimate(tm_, tn_):
        return (tm_ * k * x.dtype.itemsize + (tm_ + ms) * k * 2 + n_w * k * tn_ * (2 * 4 + 2)
                + tm_ * tn_ * (2 * out_b + (8 if has_res else 0)) + n_w * (tm_ + ms) * tn_ * 4 + 2 * MIB)

    row_blocks = [t for t in (MM_MAX_ROWS, 1024, 512, 256, 128, 64, 32, 16, 8) if m % t == 0] or [m]
    col_blocks = [t for t in (512, 256) if n % t == 0 and col0 % t == 0] or [n]
    tm, tn = next(((a, b) for a in row_blocks for b in col_blocks if vmem_estimate(a, b) <= MM_VMEM_BUDGET),
                  (row_blocks[-1], col_blocks[-1]))
    assert m % tm == 0 and n % tn == 0 and col0 % tn == 0
    jb0 = col0 // tn
    once = pl.Buffered(1)
    in_specs = [pl.BlockSpec((tm, k), lambda i, j: (i, 0), pipeline_mode=once)]
    args = [x]
    if ms:
        in_specs.append(pl.BlockSpec((ms, k), lambda i, j: (0, 0), pipeline_mode=once))
        args.append(side_x)
    if has_gain:
        in_specs.append(pl.BlockSpec((1, k), lambda i, j: (0, 0), pipeline_mode=once))
        args.append(gain.reshape(1, k).astype(f32))
    for w in ws:
        if w_t:
            in_specs.append(pl.BlockSpec((None, tn, k), lambda i, j: (layer, jb0 + j, 0)))
        elif layer is None:
            in_specs.append(pl.BlockSpec((k, tn), lambda i, j: (0, jb0 + j)))
        else:
            in_specs.append(pl.BlockSpec((None, k, tn), lambda i, j: (layer, 0, jb0 + j)))
        args.append(w)
    if has_res:
        in_specs.append(pl.BlockSpec((tm, tn), lambda i, j: (i, j)))
        args.append(res)
        if ms:
            in_specs.append(pl.BlockSpec((ms, tn), lambda i, j: (0, j)))
            args.append(side_res)
    out_specs = [pl.BlockSpec((tm, tn), lambda i, j: (i, j))]
    out_shape = [jax.ShapeDtypeStruct((m, n), out_dtype)]
    if ms:
        out_specs.append(pl.BlockSpec((ms, tn), lambda i, j: (i, j)))
        out_shape.append(jax.ShapeDtypeStruct((m // tm * ms, n), out_dtype))
    body = functools.partial(_mm_body, n_w=n_w, has_gain=has_gain, has_res=has_res, epi=epi, tm=tm, ms=ms, w_t=w_t)
    outs = pl.pallas_call(
        body,
        grid=(m // tm, n // tn),
        in_specs=in_specs,
        out_specs=out_specs,
        out_shape=out_shape,
        scratch_shapes=[pltpu.VMEM((tm + ms, k), bf16)],
        compiler_params=_cparams(2, vmem_estimate(tm, tn) + 6 * MIB),
        name="mm_" + epi,
    )(*args)
    return (outs[0], outs[1][:ms]) if ms else outs[0]


def _rms_body(x_ref, g_ref, o_ref):
    x = x_ref[...]
    o_ref[...] = x * lax.rsqrt(jnp.mean(x * x, axis=-1, keepdims=True) + NORM_EPS) * g_ref[...]


def _rms_norm(x, gain):
    m, d = x.shape
    tm = min(m, 512)
    return pl.pallas_call(
        _rms_body,
        grid=(m // tm,),
        in_specs=[pl.BlockSpec((tm, d), lambda i: (i, 0)), pl.BlockSpec((1, d), lambda i: (0, 0))],
        out_specs=pl.BlockSpec((tm, d), lambda i: (i, 0)),
        out_shape=jax.ShapeDtypeStruct((m, d), f32),
        compiler_params=_cparams(1, 4 * tm * d * 4 + 4 * MIB),
        name="final_rms",
    )(x, gain.reshape(1, d))


def _ssd_body(z_ref, xp_ref, bc_ref, dtq_ref, dtt_ref, cw_ref, cb_ref, dtb_ref, dtbt_ref, alog_ref, alogt_ref,
              dsk_ref, gn_ref, h0_ref, c0_ref, yg_ref, ssm_ref, ht_ref, xs_ref, xc_ref, *, valid_len):
    q = SSD_CHUNK
    c = pl.program_id(1)
    n_c = pl.num_programs(1)

    @pl.when(c == 0)
    def _():
        ht_ref[...] = h0_ref[0]
        xs_ref[0:V7X_SUBLANES, :] = c0_ref[0]

    xs_ref[V7X_SUBLANES:V7X_SUBLANES + q, 0:SSD_INNER] = xp_ref[0]
    xs_ref[V7X_SUBLANES:V7X_SUBLANES + q, SSD_INNER:SSD_CONV_DIM] = bc_ref[0]
    lane_step = 512
    for s in range(SSD_CONV_DIM // lane_step):
        sl = slice(s * lane_step, (s + 1) * lane_step)
        acc = cb_ref[:, sl] + cw_ref[SSD_CONV - 1:SSD_CONV, sl] * xs_ref[V7X_SUBLANES:V7X_SUBLANES + q, sl]
        for kk in range(1, SSD_CONV):
            acc = acc + cw_ref[SSD_CONV - 1 - kk:SSD_CONV - kk, sl] * xs_ref[V7X_SUBLANES - kk:V7X_SUBLANES - kk + q, sl]
        xc_ref[:, sl] = jax.nn.silu(acc)
    xs_ref[0:V7X_SUBLANES, :] = xs_ref[q:q + V7X_SUBLANES, :]

    ii = lax.broadcasted_iota(i32, (q, q), 0)
    jj = lax.broadcasted_iota(i32, (q, q), 1)
    causal = ii >= jj
    tril = jnp.where(causal, 1.0, 0.0).astype(bf16)
    triu = jnp.where(jj >= ii, 1.0, 0.0).astype(bf16)

    n_terms = 3
    hrow = lax.broadcasted_iota(i32, (n_terms * SSD_HPG, SSD_GROUP_W), 0) % SSD_HPG
    spread_x = jnp.where(lax.broadcasted_iota(i32, hrow.shape, 1) // SSD_HEAD_DIM == hrow, 1.0, 0.0).astype(bf16)
    prow = lax.broadcasted_iota(i32, (n_terms * SSD_HPG, SSD_HPG * q), 0) % SSD_HPG
    spread_q = jnp.where(lax.broadcasted_iota(i32, prow.shape, 1) // q == prow, 1.0, 0.0).astype(bf16)
    low_half = lax.broadcasted_iota(i32, (q, V7X_LANES), 1) < SSD_HEAD_DIM

    def spread(cols, mat):
        return _dot(jnp.concatenate(_split3(cols), axis=1), mat)

    def group(g):
        dt = jax.nn.softplus(dtq_ref[0, g] + dtb_ref[g])
        dtt = jax.nn.softplus(dtt_ref[0, g] + dtbt_ref[g])
        if valid_len is not None:
            dt = jnp.where(lax.broadcasted_iota(i32, dt.shape, 0) < valid_len, dt, 0.0)
            dtt = jnp.where(lax.broadcasted_iota(i32, dtt.shape, 1) < valid_len, dtt, 0.0)
        adt = dt * (-jnp.exp(alog_ref[g]))
        adtt = dtt * (-jnp.exp(alogt_ref[g]))
        acum = sum(_dot(tril, p) for p in _split3(adt))
        acumt = sum(_dot(p, triu) for p in _split3(adtt))
        last = acum[q - 1:q, :]
        col_q = spread(acum, spread_q)
        e_x = spread(jnp.exp(acum), spread_x)
        w_x = spread(jnp.exp(last - acum) * dt, spread_x)

        xoff = g * SSD_GROUP_W
        boff = SSD_INNER + g * SSD_STATE
        coff = SSD_INNER + SSD_GROUPS * SSD_STATE + g * SSD_STATE
        xg = xc_ref[:, pl.ds(xoff, SSD_GROUP_W)]
        bg = xc_ref[:, pl.ds(boff, SSD_STATE)].astype(bf16)
        cg = xc_ref[:, pl.ds(coff, SSD_STATE)].astype(bf16)
        cbm = _dot_nt(cg, bg)
        htg = ht_ref[:, pl.ds(xoff, SSD_GROUP_W)]
        yoff = _dot(cg, htg.astype(bf16))

        yds = []
        for pr in range(SSD_HPG // 2):
            xp = xg[:, pr * V7X_LANES:(pr + 1) * V7X_LANES]
            halves = (jnp.where(low_half, xp, 0.0).astype(bf16), jnp.where(low_half, 0.0, xp).astype(bf16))
            acc = None
            for r, xh in zip((2 * pr, 2 * pr + 1), halves):
                seg = col_q[:, r * q:(r + 1) * q] - acumt[r:r + 1, :]
                dec = jnp.exp(jnp.where(causal, seg, -jnp.inf))
                part = _dot((cbm * dec * dtt[r:r + 1, :]).astype(bf16), xh)
                acc = part if acc is None else acc + part
            yds.append(acc)
        yd = jnp.concatenate(yds, axis=1)
        yg = yd + yoff * e_x + dsk_ref[:, pl.ds(xoff, SSD_GROUP_W)] * xg

        ht_ref[:, pl.ds(xoff, SSD_GROUP_W)] = htg * e_x[q - 1:q, :] + _dot_tn(bg, (xg * w_x).astype(bf16))

        gz = yg * jax.nn.silu(z_ref[0, :, pl.ds(xoff, SSD_GROUP_W)])
        gzn = gz * lax.rsqrt(jnp.mean(gz * gz, axis=-1, keepdims=True) + NORM_EPS)
        yg_ref[0, :, pl.ds(xoff, SSD_GROUP_W)] = (gzn * gn_ref[:, pl.ds(xoff, SSD_GROUP_W)]).astype(yg_ref.dtype)

    for g in range(SSD_GROUPS):
        group(g)

    @pl.when(c == n_c - 1)
    def _():
        for g in range(SSD_GROUPS):
            sl = slice(g * SSD_GROUP_W, (g + 1) * SSD_GROUP_W)
            ssm_ref[0, sl, :] = ht_ref[:, sl].T


def _ssd_scan(proj, dt_raw, conv0, ssm0, conv_w, conv_b, dt_bias, a_log, d_skip, gate_norm, *, valid_len=None):
    b_, l_, _ = proj.shape
    q = SSD_CHUNK
    assert l_ % q == 0
    n_c = l_ // q
    g_, r_ = SSD_GROUPS, SSD_HPG
    dtq = dt_raw.reshape(b_, l_, g_, r_).transpose(0, 2, 1, 3)
    dtt = dt_raw.reshape(b_, l_, g_, r_).transpose(0, 2, 3, 1)
    h0t = ssm0.reshape(b_, SSD_INNER, SSD_STATE).transpose(0, 2, 1)
    c0 = jnp.pad(conv0, ((0, 0), (V7X_SUBLANES - (SSD_CONV - 1), 0), (0, 0)))
    dtb = dt_bias.reshape(g_, 1, r_)
    dtbt = dt_bias.reshape(g_, r_, 1)
    alog = a_log.reshape(g_, 1, r_)
    alogt = a_log.reshape(g_, r_, 1)
    dsk = jnp.repeat(d_skip, SSD_HEAD_DIM).reshape(1, SSD_INNER)
    full = lambda shape: pl.BlockSpec(shape, lambda b, c: (0,) * len(shape))
    in_specs = [
        pl.BlockSpec((1, q, SSD_INNER), lambda b, c: (b, c, 0)),
        pl.BlockSpec((1, q, SSD_INNER), lambda b, c: (b, c, 1)),
        pl.BlockSpec((1, q, SSD_BC_W), lambda b, c: (b, c, 2 * SSD_INNER // SSD_BC_W)),
        pl.BlockSpec((1, g_, q, r_), lambda b, c: (b, 0, c, 0)),
        pl.BlockSpec((1, g_, r_, q), lambda b, c: (b, 0, 0, c)),
        full((SSD_CONV, SSD_CONV_DIM)), full((1, SSD_CONV_DIM)),
        full((g_, 1, r_)), full((g_, r_, 1)), full((g_, 1, r_)), full((g_, r_, 1)), full((1, SSD_INNER)),
        full((1, SSD_INNER)),
        pl.BlockSpec((1, SSD_STATE, SSD_INNER), lambda b, c: (b, 0, 0)),
        pl.BlockSpec((1, V7X_SUBLANES, SSD_CONV_DIM), lambda b, c: (b, 0, 0)),
    ]
    out_specs = [
        pl.BlockSpec((1, q, SSD_INNER), lambda b, c: (b, c, 0)),
        pl.BlockSpec((1, SSD_INNER, SSD_STATE), lambda b, c: (b, 0, 0)),
    ]
    yg, ssm = pl.pallas_call(
        functools.partial(_ssd_body, valid_len=valid_len),
        grid=(b_, n_c),
        in_specs=in_specs,
        out_specs=out_specs,
        out_shape=[jax.ShapeDtypeStruct((b_, l_, SSD_INNER), bf16),
                   jax.ShapeDtypeStruct((b_, SSD_INNER, SSD_STATE), f32)],
        scratch_shapes=[pltpu.VMEM((SSD_STATE, SSD_INNER), f32),
                        pltpu.VMEM((q + 2 * V7X_SUBLANES, SSD_CONV_DIM), f32),
                        pltpu.VMEM((q, SSD_CONV_DIM), f32)],
        compiler_params=_cparams(2, 48 * MIB),
        name="ssd_scan",
    )(proj, proj, proj, dtq, dtt, conv_w, conv_b.reshape(1, -1), dtb, dtbt, alog, alogt, dsk,
      gate_norm.reshape(1, -1), h0t, c0)
    return yg, ssm.reshape(b_, SSD_HEADS, SSD_HEAD_DIM, SSD_STATE)


def _trig_body(pos_ref, invf_ref, sgn_ref, cos_ref, sin_ref):
    ang = pos_ref[...].astype(f32) * invf_ref[...]
    rot = lax.broadcasted_iota(i32, ang.shape, 1) < ROPE_DIM
    cos_ref[...] = jnp.where(rot, jnp.cos(ang), 1.0)
    sin_ref[...] = jnp.sin(ang) * sgn_ref[...]


def _trig_tables(pos):
    n = pos.shape[0]
    inv_freq = ROPE_THETA ** (-jnp.arange(ROPE_HALF, dtype=f32) * 2.0 / ROPE_DIM)
    pad = jnp.zeros((MOBA_HEAD_DIM - ROPE_DIM,), f32)
    invf = jnp.concatenate([inv_freq, inv_freq, pad]).reshape(1, MOBA_HEAD_DIM)
    sgn = jnp.concatenate([-jnp.ones((ROPE_HALF,), f32), jnp.ones((ROPE_HALF,), f32), pad]).reshape(1, MOBA_HEAD_DIM)
    out = jax.ShapeDtypeStruct((n, MOBA_HEAD_DIM), f32)
    return pl.pallas_call(_trig_body, out_shape=[out, out], name="rope_tables")(pos.reshape(n, 1).astype(i32), invf, sgn)


def _rotate(x, cos_t, sin_t):
    lane = lax.broadcasted_iota(i32, x.shape, 1)
    partner = jnp.where(lane < ROPE_HALF, pltpu.roll(x, MOBA_HEAD_DIM - ROPE_HALF, 1), pltpu.roll(x, ROPE_HALF, 1))
    return x * cos_t + partner * sin_t


def _prep_body(qkv_ref, cos_ref, sin_ref, q_ref, k_ref, kb_ref, vb_ref, *maybe_kmean, with_kmean):
    cos_t = cos_ref[...]
    sin_t = sin_ref[...]
    for h in range(MOBA_HEADS):
        sl = slice(h * MOBA_HEAD_DIM, (h + 1) * MOBA_HEAD_DIM)
        q_ref[:, sl] = _rotate(qkv_ref[:, sl], cos_t, sin_t)
        kr = _rotate(qkv_ref[:, ATTN_W + h * MOBA_HEAD_DIM:ATTN_W + (h + 1) * MOBA_HEAD_DIM], cos_t, sin_t)
        k_ref[:, sl] = kr
        kb_ref[:, sl] = kr.astype(bf16)
        if with_kmean:
            maybe_kmean[0][0, :, sl] = jnp.sum(kr, axis=0, keepdims=True) * (1.0 / MOBA_BLOCK)
    vb_ref[...] = qkv_ref[:, 2 * ATTN_W:3 * ATTN_W].astype(bf16)


def _moba_prep(qkv, cos_t, sin_t, *, rows, with_kmean):
    m = qkv.shape[0]
    n_tab = cos_t.shape[0] // rows
    row_spec = lambda w: pl.BlockSpec((rows, w), lambda i: (i, 0))
    tab_spec = pl.BlockSpec((rows, MOBA_HEAD_DIM), lambda i: (i % n_tab, 0))
    out_shape = [jax.ShapeDtypeStruct((m, ATTN_W), f32), jax.ShapeDtypeStruct((m, ATTN_W), f32),
                 jax.ShapeDtypeStruct((m, ATTN_W), bf16), jax.ShapeDtypeStruct((m, ATTN_W), bf16)]
    out_specs = [row_spec(ATTN_W)] * 4
    if with_kmean:
        out_shape.append(jax.ShapeDtypeStruct((m // rows, 1, ATTN_W), f32))
        out_specs.append(pl.BlockSpec((1, 1, ATTN_W), lambda i: (i, 0, 0)))
    return pl.pallas_call(
        functools.partial(_prep_body, with_kmean=with_kmean),
        grid=(m // rows,),
        in_specs=[row_spec(3 * ATTN_W), tab_spec, tab_spec],
        out_specs=out_specs,
        out_shape=out_shape,
        compiler_params=_cparams(1, 40 * MIB),
        name="moba_prep",
    )(qkv, cos_t, sin_t)


def _gate_scores(q, km):
    qh, qm, _ = _split3(q)
    kh, kmid, _ = _split3(km)
    return _dot_nt(qh, kh) + _dot_nt(qh, kmid) + _dot_nt(qm, kh)


def _top_blocks(gate, n_take):
    lane = lax.broadcasted_iota(i32, gate.shape, 1)
    picks = []
    for _ in range(n_take):
        mx = jnp.max(gate, axis=-1, keepdims=True)
        idx = jnp.min(jnp.where(gate == mx, lane, gate.shape[1]), axis=-1, keepdims=True)
        picks.append(idx)
        gate = jnp.where(lane == idx, -jnp.inf, gate)
    return picks


ATTN_HEADS_PER_STEP = 4
ATTN_MASKED = -1e30


def _attn_prompt_body(q_ref, kb_ref, vb_ref, km_ref, o_ref, s_ref, qa_ref):
    blk = MOBA_BLOCK
    hd = MOBA_HEAD_DIM
    cur = pl.program_id(2)
    n_blk = km_ref.shape[1]
    n_all = n_blk * blk
    scale = hd ** -0.5
    blk_id = lax.broadcasted_iota(i32, (n_blk, blk), 0)
    past = blk_id < cur
    causal = lax.broadcasted_iota(i32, (blk, blk), 1) <= lax.broadcasted_iota(i32, (blk, blk), 0)
    ones = jnp.ones((blk, hd), bf16)
    lane = lax.broadcasted_iota(i32, (blk, hd), 1)
    eye = jnp.where(lax.broadcasted_iota(i32, (n_blk, hd), 1) == lax.broadcasted_iota(i32, (n_blk, hd), 0),
                    1.0, 0.0).astype(bf16)
    own = pl.multiple_of(cur * blk, blk)
    own_slot = slice(n_all, n_all + blk)

    for hh in range(ATTN_HEADS_PER_STEP):
        hs = slice(hh * hd, (hh + 1) * hd)
        q = q_ref[:, hs]
        gate = jnp.where(past, _gate_scores(km_ref[0, :, hs], q), -jnp.inf)
        sel = jnp.zeros((n_blk, blk), i32)
        for _ in range(min(MOBA_TOPK, n_blk)):
            mx = jnp.max(gate, axis=0, keepdims=True)
            idx = jnp.min(jnp.where(gate == mx, blk_id, n_blk), axis=0, keepdims=True)
            sel = jnp.where((blk_id == idx) & past, 1, sel)
            gate = jnp.where(blk_id == idx, -jnp.inf, gate)
        bias_t = jnp.where(sel > 0, 0.0, ATTN_MASKED).astype(bf16)
        qa_ref[hh, :, 0:hd] = (q * scale).astype(bf16)
        qa_ref[hh, :, hd:2 * hd] = _dot_tn(bias_t, eye).astype(bf16)

    def attend(n_past):
        for hh in range(ATTN_HEADS_PER_STEP):
            hs = slice(hh * hd, (hh + 1) * hd)
            q_aug = qa_ref[hh]
            s = jnp.where(causal, _dot_nt(q_aug[:, 0:hd], kb_ref[pl.ds(own, blk), hs]), -jnp.inf)
            s_ref[hh, :, own_slot] = s
            m2 = jnp.maximum(s[:, :hd], s[:, hd:])
            for n in range(n_past):
                keys = slice(n * blk, (n + 1) * blk)
                k_aug = jnp.concatenate([kb_ref[keys, hs], jnp.where(lane == n, 1.0, 0.0).astype(bf16)], axis=1)
                s = _dot_nt(q_aug, k_aug)
                s_ref[hh, :, keys] = s
                m2 = jnp.maximum(m2, jnp.maximum(s[:, :hd], s[:, hd:]))
            m = jnp.broadcast_to(jnp.max(m2, axis=-1, keepdims=True), (blk, blk))
            p = jnp.exp(s_ref[hh, :, own_slot] - m).astype(bf16)
            acc = _dot(p, jnp.concatenate([vb_ref[pl.ds(own, blk), hs], ones], axis=1))
            for n in range(n_past):
                keys = slice(n * blk, (n + 1) * blk)
                p = jnp.exp(s_ref[hh, :, keys] - m).astype(bf16)
                acc = acc + _dot(p, jnp.concatenate([vb_ref[keys, hs], ones], axis=1))
            o_ref[:, hs] = (acc[:, :hd] / acc[:, hd:]).astype(o_ref.dtype)

    for pair in range((n_blk + 1) // 2):
        pl.when(cur // 2 == pair)(functools.partial(attend, min(2 * pair + 1, n_blk - 1)))


def _moba_attn_prompt(q, kb, vb, kmean, b_, l_):
    n_blk = l_ // MOBA_BLOCK
    w = ATTN_HEADS_PER_STEP * MOBA_HEAD_DIM
    return pl.pallas_call(
        _attn_prompt_body,
        grid=(b_, MOBA_HEADS // ATTN_HEADS_PER_STEP, n_blk),
        in_specs=[
            pl.BlockSpec((MOBA_BLOCK, w), lambda b, h, t: (b * n_blk + t, h)),
            pl.BlockSpec((l_, w), lambda b, h, t: (b, h)),
            pl.BlockSpec((l_, w), lambda b, h, t: (b, h)),
            pl.BlockSpec((1, n_blk, w), lambda b, h, t: (b, 0, h)),
        ],
        out_specs=pl.BlockSpec((MOBA_BLOCK, w), lambda b, h, t: (b * n_blk + t, h)),
        out_shape=jax.ShapeDtypeStruct((b_ * l_, ATTN_W), bf16),
        scratch_shapes=[pltpu.VMEM((ATTN_HEADS_PER_STEP, MOBA_BLOCK, l_ + MOBA_BLOCK), f32),
                        pltpu.VMEM((ATTN_HEADS_PER_STEP, MOBA_BLOCK, 2 * MOBA_HEAD_DIM), bf16)],
        compiler_params=_cparams(3, 32 * MIB),
        name="moba_attn_prompt",
    )(q, kb, vb, kmean)


PAGES_PER_BLOCK = MOBA_BLOCK // PAGE_SIZE
KMEAN_BLOCKS_PER_STEP = 8
KMEAN_PAGES_PER_STEP = KMEAN_BLOCKS_PER_STEP * PAGES_PER_BLOCK


def _kmean_body(pt_ref, *refs):
    page_refs, o_ref = refs[:-1], refs[-1]
    for blk in range(KMEAN_BLOCKS_PER_STEP):
        tot = None
        for p in range(PAGES_PER_BLOCK):
            part = jnp.sum(page_refs[blk * PAGES_PER_BLOCK + p][...], axis=0)
            tot = part if tot is None else tot + part
        tot = tot * (1.0 / MOBA_BLOCK)
        for h in range(MOBA_HEADS):
            o_ref[0, blk:blk + 1, h * MOBA_HEAD_DIM:(h + 1) * MOBA_HEAD_DIM] = tot[h:h + 1, :]


def _cache_kmean(cache, layer, page_table):
    b_, n_pages = page_table.shape
    n_steps = n_pages // KMEAN_PAGES_PER_STEP
    pt_flat = page_table.reshape(-1)

    def page_spec(p):
        return pl.BlockSpec((None, None, PAGE_SIZE, MOBA_HEADS, MOBA_HEAD_DIM),
                            lambda b, s, pt: (layer, pt[b * n_pages + s * KMEAN_PAGES_PER_STEP + p], 0, 0, 0))

    return pl.pallas_call(
        _kmean_body,
        grid_spec=pltpu.PrefetchScalarGridSpec(
            num_scalar_prefetch=1,
            grid=(b_, n_steps),
            in_specs=[page_spec(p) for p in range(KMEAN_PAGES_PER_STEP)],
            out_specs=pl.BlockSpec((1, KMEAN_BLOCKS_PER_STEP, ATTN_W), lambda b, s, pt: (b, s, 0)),
        ),
        out_shape=jax.ShapeDtypeStruct((b_, n_pages // PAGES_PER_BLOCK, ATTN_W), f32),
        compiler_params=_cparams(2, 40 * MIB),
        name="cache_kmean",
    )(pt_flat, *([cache] * KMEAN_PAGES_PER_STEP))


def _sample_topk_body(q_ref, km_ref, o_ref, g_ref):
    rows = q_ref.shape[1]
    for h in range(MOBA_HEADS):
        sl = slice(h * MOBA_HEAD_DIM, (h + 1) * MOBA_HEAD_DIM)
        g_ref[h * rows:(h + 1) * rows, :] = _gate_scores(q_ref[0, :, sl], km_ref[0, :, sl])
    picks = _top_blocks(g_ref[...], MOBA_TOPK)
    lane = lax.broadcasted_iota(i32, o_ref.shape[1:], 1)
    out = jnp.zeros(o_ref.shape[1:], i32)
    for s, idx in enumerate(picks):
        out = jnp.where(lane == s, idx, out)
    o_ref[0] = out


def _sample_topk(q8, kmean):
    b_, rows, _ = q8.shape
    n_blk = kmean.shape[1]
    return pl.pallas_call(
        _sample_topk_body,
        grid=(b_,),
        in_specs=[pl.BlockSpec((1, rows, ATTN_W), lambda b: (b, 0, 0)),
                  pl.BlockSpec((1, n_blk, ATTN_W), lambda b: (b, 0, 0))],
        out_specs=pl.BlockSpec((1, MOBA_HEADS * rows, V7X_LANES), lambda b: (b, 0, 0)),
        out_shape=jax.ShapeDtypeStruct((b_, MOBA_HEADS * rows, V7X_LANES), i32),
        scratch_shapes=[pltpu.VMEM((MOBA_HEADS * rows, n_blk), f32)],
        compiler_params=_cparams(1, 16 * MIB),
        name="sample_topk",
    )(q8, kmean)


def _attn_sample_body(sel_ref, pt_ref, q_ref, kn_ref, vn_ref, ck_hbm, cv_hbm, o_ref, kbuf, vbuf, sems,
                      *, layer, n_tok, n_pages):
    n_sel = n_tok * MOBA_TOPK * PAGES_PER_BLOCK
    n_heads = pl.num_programs(1)
    step = pl.program_id(0) * n_heads + pl.program_id(1)
    n_steps = pl.num_programs(0) * n_heads
    slot = lax.rem(step, 2)

    def page_copies(step_, slot_):
        b_ = step_ // n_heads
        h_ = lax.rem(step_, n_heads)
        out = []
        for i in range(n_sel):
            ts, p = divmod(i, PAGES_PER_BLOCK)
            blk = sel_ref[step_ * (n_tok * MOBA_TOPK) + ts]
            page = pt_ref[b_ * n_pages + blk * PAGES_PER_BLOCK + p]
            out.append(pltpu.make_async_copy(ck_hbm.at[layer, page, :, h_, :], kbuf.at[slot_, i], sems.at[0, slot_]))
            out.append(pltpu.make_async_copy(cv_hbm.at[layer, page, :, h_, :], vbuf.at[slot_, i], sems.at[1, slot_]))
        return out

    @pl.when(step == 0)
    def _():
        for cp in page_copies(step, slot):
            cp.start()

    @pl.when(step + 1 < n_steps)
    def _():
        for cp in page_copies(step + 1, 1 - slot):
            cp.start()

    for cp in page_copies(step, slot):
        cp.wait()

    rows = q_ref.shape[1]
    scale = MOBA_HEAD_DIM ** -0.5
    qb = q_ref[0].astype(bf16)
    row = lax.broadcasted_iota(i32, (rows, PAGE_SIZE), 0)
    k_refs = [kbuf.at[slot, i] for i in range(n_sel)]
    v_refs = [vbuf.at[slot, i] for i in range(n_sel)]
    scores = []
    for i in range(n_sel):
        t = i // (MOBA_TOPK * PAGES_PER_BLOCK)
        s = _dot_nt(qb, k_refs[i][...].astype(bf16)) * scale
        scores.append(jnp.where(row == t, s, -jnp.inf))
    r8 = lax.broadcasted_iota(i32, (rows, rows), 0)
    c8 = lax.broadcasted_iota(i32, (rows, rows), 1)
    s_own = _dot_nt(qb, kn_ref[0]) * scale
    s_own = jnp.where((c8 <= r8) & (c8 < n_tok), s_own, -jnp.inf)
    m = jnp.max(s_own, axis=-1, keepdims=True)
    for s in scores:
        m = jnp.maximum(m, jnp.max(s, axis=-1, keepdims=True))
    m = jnp.where(m == -jnp.inf, 0.0, m)
    p_own = jnp.exp(s_own - m)
    l = jnp.sum(p_own, axis=-1, keepdims=True)
    acc = _dot(p_own.astype(bf16), vn_ref[0])
    for i in range(n_sel):
        p = jnp.exp(scores[i] - m)
        l = l + jnp.sum(p, axis=-1, keepdims=True)
        acc = acc + _dot(p.astype(bf16), v_refs[i][...].astype(bf16))
    l = jnp.where(l == 0.0, 1.0, l)
    o_ref[0] = (acc / l).astype(o_ref.dtype)


def _moba_attn_sample(sel, page_table, q8, kn8, vn8, cache_k, cache_v, layer, n_tok):
    b_, rows, _ = q8.shape
    n_pages = page_table.shape[1]
    sel_flat = sel.reshape(-1).astype(i32)
    pt_flat = page_table.reshape(-1).astype(i32)
    head_spec = pl.BlockSpec((1, rows, MOBA_HEAD_DIM), lambda b, h, sel_r, pt_r: (b, 0, h))
    n_sel = n_tok * MOBA_TOPK * PAGES_PER_BLOCK
    page_buf = pltpu.VMEM((2, n_sel, PAGE_SIZE, MOBA_HEAD_DIM), f32)
    return pl.pallas_call(
        functools.partial(_attn_sample_body, layer=layer, n_tok=n_tok, n_pages=n_pages),
        grid_spec=pltpu.PrefetchScalarGridSpec(
            num_scalar_prefetch=2,
            grid=(b_, MOBA_HEADS),
            in_specs=[head_spec, head_spec, head_spec,
                      pl.BlockSpec(memory_space=pl.ANY), pl.BlockSpec(memory_space=pl.ANY)],
            out_specs=head_spec,
            scratch_shapes=[page_buf, page_buf, pltpu.SemaphoreType.DMA((2, 2))],
        ),
        out_shape=jax.ShapeDtypeStruct((b_, rows, ATTN_W), bf16),
        compiler_params=_cparams(2, 24 * MIB),
        name="moba_attn_sample",
    )(sel_flat, pt_flat, q8, kn8, vn8, cache_k, cache_v)


def _gmlp_body(uv_ref, lg_ref, lb_ref, ws_ref, bst_ref, o_ref, *maybe_v, with_v):
    n = GMLP_CHUNK
    vf = uv_ref[:, GMLP_DIM:2 * GMLP_DIM]
    mu = jnp.mean(vf, axis=-1, keepdims=True)
    var = jnp.mean(jnp.square(vf - mu), axis=-1, keepdims=True)
    v = (vf - mu) * lax.rsqrt(var + NORM_EPS) * lg_ref[...] + lb_ref[...]
    if with_v:
        maybe_v[0][...] = v
    ii = lax.broadcasted_iota(i32, (n, n), 0)
    jj = lax.broadcasted_iota(i32, (n, n), 1)
    for g in range(GMLP_GROUPS):
        sl = slice(g * GMLP_GROUP_DIM, (g + 1) * GMLP_GROUP_DIM)
        w = jnp.where(jj <= ii, ws_ref[g], 0.0).astype(bf16)
        s = _dot(w, v[:, sl].astype(bf16)) + bst_ref[:, g:g + 1]
        o_ref[:, sl] = (uv_ref[:, sl] * s).astype(o_ref.dtype)


def _gmlp_gate(uv, ln_g, ln_b, w_s, b_s, *, with_v):
    m = uv.shape[0]
    n = GMLP_CHUNK
    out_shape = [jax.ShapeDtypeStruct((m, GMLP_DIM), bf16)]
    out_specs = [pl.BlockSpec((n, GMLP_DIM), lambda i: (i, 0))]
    if with_v:
        out_shape.append(jax.ShapeDtypeStruct((m, GMLP_DIM), f32))
        out_specs.append(pl.BlockSpec((n, GMLP_DIM), lambda i: (i, 0)))
    return pl.pallas_call(
        functools.partial(_gmlp_body, with_v=with_v),
        grid=(m // n,),
        in_specs=[pl.BlockSpec((n, 2 * GMLP_DIM), lambda i: (i, 0)),
                  pl.BlockSpec((1, GMLP_DIM), lambda i: (0, 0)),
                  pl.BlockSpec((1, GMLP_DIM), lambda i: (0, 0)),
                  pl.BlockSpec((GMLP_GROUPS, n, n), lambda i: (0, 0, 0)),
                  pl.BlockSpec((n, GMLP_GROUPS), lambda i: (0, 0))],
        out_specs=out_specs,
        out_shape=out_shape,
        compiler_params=_cparams(1, 24 * MIB),
        name="gmlp_gate",
    )(uv, ln_g.reshape(1, -1), ln_b.reshape(1, -1), w_s, b_s.T)


def _pad_rows(x, rows):
    return jnp.pad(x, ((0, 0), (0, rows - x.shape[1]), (0, 0)))


def _ssd_group(proj, dt_raw, b_, l_, conv0, ssm0, scan_w):
    proj3 = proj.reshape(b_, l_, SSD_MAIN_W)
    dt3 = dt_raw.reshape(b_, l_, SSD_HEADS)
    if l_ < SSD_CONV - 1:
        new_conv = jnp.concatenate([conv0, proj3[:, :, SSD_INNER:]], axis=1)[:, -(SSD_CONV - 1):]
    else:
        new_conv = proj3[:, l_ - (SSD_CONV - 1):, SSD_INNER:]
    lp = -(-l_ // SSD_CHUNK) * SSD_CHUNK
    valid_len = None
    if lp != l_:
        assert lp == SSD_CHUNK
        proj3 = _pad_rows(proj3, lp)
        dt3 = _pad_rows(dt3, lp)
        valid_len = l_
    yg, ssm = _ssd_scan(proj3, dt3, conv0, ssm0, *scan_w, valid_len=valid_len)
    if lp != l_:
        yg = yg[:, :l_]
    return yg.reshape(b_ * l_, SSD_INNER), new_conv, ssm


def _ssd_layer(hp, hs, bp, lp_, bs, ls, gain, conv0_p, ssm0_p, conv0_s, ssm0_s, j, w_in_t, scan_w, w_out):
    proj_p, proj_s = _mm(hp, [w_in_t], layer=j, w_t=True, n_out=SSD_MAIN_W, gain=gain, side_x=hs)
    dt_p, dt_s = _mm(hp, [w_in_t], layer=j, w_t=True, col0=SSD_MAIN_W, gain=gain, side_x=hs)
    yg_p, conv_p, ssm_p = _ssd_group(proj_p, dt_p, bp, lp_, conv0_p, ssm0_p, scan_w)
    yg_s, conv_s, ssm_s = _ssd_group(proj_s, dt_s, bs, ls, conv0_s, ssm0_s, scan_w)
    hp, hs = _mm(yg_p, [w_out], layer=j, res=hp, side_x=yg_s, side_res=hs)
    return hp, hs, (conv_p, ssm_p), (conv_s, ssm_s)


def _ffn_layer(hp, hs, i, gain, w_gate, w_up, w_down):
    tp, ts = _mm(hp, [w_gate, w_up], layer=i, gain=gain, epi="swiglu", out_dtype=bf16, side_x=hs)
    return _mm(tp, [w_down], layer=i, res=hp, side_x=ts, side_res=hs)


def kernel(x_prompt, x_sample, state_ssm, state_conv, cache_k, cache_v, page_table, norm_mix, norm_ffn, norm_final, ssd_w_in, ssd_conv_w, ssd_conv_b, ssd_dt_bias, ssd_a_log, ssd_d, ssd_gate_norm, ssd_w_out, moba_w_qkv, moba_w_o, gmlp_w_in, gmlp_ln_g, gmlp_ln_b, gmlp_w_s, gmlp_b_s, gmlp_w_out, ffn_w_gate, ffn_w_up, ffn_w_down):
    bp, lp_, d = x_prompt.shape
    bs, ls, _ = x_sample.shape
    hp = x_prompt.reshape(bp * lp_, d)
    hs = x_sample.reshape(bs * ls, d)
    n_pages = page_table.shape[1]
    past = n_pages * PAGE_SIZE
    tok_rows = V7X_SUBLANES
    assert ls <= tok_rows and past % MOBA_BLOCK == 0 and lp_ % MOBA_BLOCK == 0

    ssm_p, ssm_s, conv_p, conv_s = [], [], [], []
    k_p, v_p, k_s, v_s, gv_s = [], [], [], [], []
    for i in range(DEPTH):
        kind = i % N_MIXERS
        j = i // N_MIXERS
        if kind == 0:
            w_in_t = jnp.swapaxes(ssd_w_in, 1, 2)
            scan_w = (ssd_conv_w[j], ssd_conv_b[j], ssd_dt_bias[j], ssd_a_log[j], ssd_d[j], ssd_gate_norm[j])
            zero_conv = jnp.zeros((bp, SSD_CONV - 1, SSD_CONV_DIM), f32)
            zero_ssm = jnp.zeros((bp, SSD_HEADS, SSD_HEAD_DIM, SSD_STATE), f32)
            hp, hs, (cp_, sp_), (cs_, ss_) = _ssd_layer(hp, hs, bp, lp_, bs, ls, norm_mix[i], zero_conv, zero_ssm,
                                                        state_conv[j], state_ssm[j], j, w_in_t, scan_w, ssd_w_out)
            conv_p.append(cp_)
            ssm_p.append(sp_)
            conv_s.append(cs_)
            ssm_s.append(ss_)
        elif kind == 1:
            qkv, qkv_s = _mm(hp, [moba_w_qkv], layer=j, gain=norm_mix[i], side_x=hs)
            cos_t, sin_t = _trig_tables(jnp.arange(lp_, dtype=i32))
            q, k, kb, vb, kmean = _moba_prep(qkv, cos_t, sin_t, rows=MOBA_BLOCK, with_kmean=True)
            o = _moba_attn_prompt(q, kb, vb, kmean.reshape(bp, lp_ // MOBA_BLOCK, ATTN_W), bp, lp_)
            k_p.append(k.reshape(bp, lp_, MOBA_HEADS, MOBA_HEAD_DIM))
            v_p.append(qkv[:, 2 * ATTN_W:].reshape(bp, lp_, MOBA_HEADS, MOBA_HEAD_DIM))
            pos_s = past + (jnp.arange(bs * ls, dtype=i32) % ls)
            cos_s, sin_s = _trig_tables(pos_s)
            q_s, kn, knb, vnb = _moba_prep(qkv_s, cos_s, sin_s, rows=bs * ls, with_kmean=False)
            kmean_c = _cache_kmean(cache_k, j, page_table)
            q8 = _pad_rows(q_s.reshape(bs, ls, ATTN_W), tok_rows)
            kn8 = _pad_rows(knb.reshape(bs, ls, ATTN_W), tok_rows)
            vn8 = _pad_rows(vnb.reshape(bs, ls, ATTN_W), tok_rows)
            top = _sample_topk(q8, kmean_c)
            sel = top.reshape(bs, MOBA_HEADS, tok_rows, V7X_LANES)[:, :, :ls, :MOBA_TOPK]
            o_s = _moba_attn_sample(sel, page_table, q8, kn8, vn8, cache_k, cache_v, j, ls)
            k_s.append(kn.reshape(bs, ls, MOBA_HEADS, MOBA_HEAD_DIM))
            v_s.append(qkv_s[:, 2 * ATTN_W:].reshape(bs, ls, MOBA_HEADS, MOBA_HEAD_DIM))
            hp, hs = _mm(o, [moba_w_o], layer=j, res=hp, side_x=o_s[:, :ls].reshape(bs * ls, ATTN_W), side_res=hs)
        else:
            w = (gmlp_ln_g[j], gmlp_ln_b[j], gmlp_w_s[j], gmlp_b_s[j])
            uv, uv_s = _mm(hp, [gmlp_w_in], layer=j, gain=norm_mix[i], epi="gelu", side_x=hs)
            (gated,) = _gmlp_gate(uv, *w, with_v=False)
            uv_s = _pad_rows(uv_s.reshape(bs, ls, 2 * GMLP_DIM), GMLP_CHUNK).reshape(bs * GMLP_CHUNK, 2 * GMLP_DIM)
            gated_s, v_rows = _gmlp_gate(uv_s, *w, with_v=True)
            gated_s = gated_s.reshape(bs, GMLP_CHUNK, GMLP_DIM)[:, :ls].reshape(bs * ls, GMLP_DIM)
            gv_s.append(v_rows.reshape(bs, GMLP_CHUNK, GMLP_DIM)[:, :ls])
            hp, hs = _mm(gated, [gmlp_w_out], layer=j, res=hp, side_x=gated_s, side_res=hs)
        hp, hs = _ffn_layer(hp, hs, i, norm_ffn[i], ffn_w_gate, ffn_w_up, ffn_w_down)
    y_prompt = _rms_norm(hp, norm_final).reshape(bp, lp_, d)
    y_sample = _rms_norm(hs, norm_final).reshape(bs, ls, d)
    return (y_prompt, y_sample, jnp.stack(ssm_p), jnp.stack(ssm_s), jnp.stack(conv_p), jnp.stack(conv_s),
            jnp.stack(k_p), jnp.stack(v_p), jnp.stack(k_s), jnp.stack(v_s), jnp.stack(gv_s))
```

```python
import functools
import math

import jax
import jax.numpy as jnp
from jax import lax
from jax.experimental import pallas as pl
from jax.experimental.pallas import tpu as pltpu

f32 = jnp.float32
bf16 = jnp.bfloat16
i32 = jnp.int32

D_MODEL = 2048
DEPTH = 4
PAGE_SIZE = 128
N_MIXERS = 3
NORM_EPS = 1e-6
SSD_INNER = 4096
SSD_HEAD_DIM = 64
SSD_HEADS = 64
SSD_GROUPS = 8
SSD_HPG = 8
SSD_STATE = 128
SSD_CONV = 4
SSD_CHUNK = 128
SSD_GROUP_W = SSD_HPG * SSD_HEAD_DIM
SSD_BC_W = 2 * SSD_GROUPS * SSD_STATE
SSD_CONV_DIM = SSD_INNER + SSD_BC_W
SSD_MAIN_W = SSD_INNER + SSD_CONV_DIM
MOBA_HEADS = 16
MOBA_HEAD_DIM = 128
MOBA_BLOCK = 256
MOBA_TOPK = 3
ROPE_THETA = 500000.0
ROPE_DIM = 32
ROPE_HALF = ROPE_DIM // 2
ATTN_W = MOBA_HEADS * MOBA_HEAD_DIM
GMLP_DIM = 2048
GMLP_GROUPS = 16
GMLP_GROUP_DIM = 128
GMLP_CHUNK = 128

V7X_LANES = 128
V7X_SUBLANES = 8
V7X_VMEM_LIMIT_BYTES = 60 * 1024 * 1024
MIB = 1024 * 1024


def _cparams(n_grid, vmem_bytes):
    return pltpu.CompilerParams(
        dimension_semantics=("arbitrary",) * n_grid,
        vmem_limit_bytes=int(min(max(vmem_bytes, 16 * MIB), V7X_VMEM_LIMIT_BYTES)),
    )


def _split3(x):
    hi = x.astype(bf16)
    r1 = x - hi.astype(f32)
    mid = r1.astype(bf16)
    lo = (r1 - mid.astype(f32)).astype(bf16)
    return hi, mid, lo


def _dot(a, b):
    return jnp.dot(a, b, preferred_element_type=f32)


def _dot_nt(a, b):
    return lax.dot_general(a, b, (((1,), (1,)), ((), ())), preferred_element_type=f32)


def _dot_tn(a, b):
    return lax.dot_general(a, b, (((0,), (0,)), ((), ())), preferred_element_type=f32)


def _gelu_exact(x):
    return 0.5 * x * (1.0 + lax.erf(x * (1.0 / math.sqrt(2.0))))


MM_MAX_ROWS = 2048
MM_VMEM_BUDGET = 50 * MIB


def _mm_body(*refs, n_w, has_gain, has_res, epi, tm, ms, w_t, staged, tail):
    it = iter(refs)
    x_ref = next(it)
    sx_ref = next(it) if ms else None
    g_ref = next(it) if has_gain else None
    w_refs = [next(it) for _ in range(n_w)]
    wt_ref = next(it) if tail else None
    r_ref = next(it) if has_res else None
    sr_ref = next(it) if (has_res and ms) else None
    o_ref = next(it)
    so_ref = next(it) if ms else None
    ot_ref = next(it) if tail else None
    sot_ref = next(it) if (tail and ms) else None
    xs_ref = next(it) if staged else None
    i = pl.program_id(0)
    j = pl.program_id(1)
    mxu = _dot_nt if w_t else _dot

    if staged:
        def prep(v):
            v = v.astype(f32)
            if has_gain:
                v = v * lax.rsqrt(jnp.mean(v * v, axis=-1, keepdims=True) + NORM_EPS) * g_ref[...]
            return v.astype(bf16)

        @pl.when(j == 0)
        def _():
            xs_ref[0:tm, :] = prep(x_ref[...])

        if ms:
            @pl.when((j == 0) & (i == 0))
            def _():
                xs_ref[tm:tm + ms, :] = prep(sx_ref[...])

    wb = [w[...].astype(bf16) for w in w_refs]

    def product(xb):
        accs = [mxu(xb, w) for w in wb]
        if epi == "swiglu":
            return jax.nn.silu(accs[0]) * accs[1]
        if epi == "gelu":
            return _gelu_exact(accs[0])
        return accs[0]

    def store(main, side):
        if has_res:
            main = r_ref[...] + main
        o_ref[...] = main.astype(o_ref.dtype)
        if side is not None:
            if has_res:
                side = sr_ref[...] + side
            so_ref[...] = side.astype(so_ref.dtype)

    def run(with_side):
        if staged:
            rows = tm + ms if with_side else tm
            out = product(xs_ref[0:rows, :])
            store(out[0:tm], out[tm:rows] if with_side else None)
            if tail:
                @pl.when(j == 0)
                def _():
                    t = _dot_nt(xs_ref[0:rows, :], wt_ref[...].astype(bf16))
                    ot_ref[...] = t[0:tm]
                    if with_side:
                        sot_ref[...] = t[tm:rows]
        else:
            store(product(x_ref[...]), product(sx_ref[...]) if with_side else None)

    if ms:
        pl.when(i == 0)(functools.partial(run, True))

        @pl.when(i > 0)
        def _():
            run(False)
            so_ref[...] = jnp.zeros(so_ref.shape, so_ref.dtype)
            if tail:
                @pl.when(j == 0)
                def _():
                    sot_ref[...] = jnp.zeros(sot_ref.shape, sot_ref.dtype)
    else:
        run(False)


def _mm(x, ws, *, layer=None, w_t=False, n_out=None, tail=0, gain=None, res=None, side_x=None, side_res=None,
        epi="none", out_dtype=f32):
    m, k = x.shape
    n_total = ws[0].shape[-2] if w_t else ws[0].shape[-1]
    n = n_out if n_out is not None else n_total
    ms = 0 if side_x is None else side_x.shape[0]
    n_w = len(ws)
    has_gain = gain is not None
    has_res = res is not None
    staged = has_gain or x.dtype != bf16
    assert not tail or (w_t and staged and n % tail == 0)
    out_b = jnp.dtype(out_dtype).itemsize
    x_bufs = 1 if staged else 2

    def vmem_estimate(tm_, tn_):
        return (x_bufs * tm_ * k * x.dtype.itemsize + (staged * (tm_ + ms) * k * 2) + n_w * k * tn_ * (2 * 4 + 2)
                + tm_ * tn_ * (2 * out_b + (8 if has_res else 0)) + n_w * (tm_ + ms) * tn_ * 4 + 2 * MIB)

    row_blocks = [t for t in (MM_MAX_ROWS, 1024, 512, 256, 128, 64, 32, 16, 8) if m % t == 0] or [m]
    col_blocks = [t for t in (512, 256) if n % t == 0] or [n]
    tm, tn = next(((a, b) for a in row_blocks for b in col_blocks if vmem_estimate(a, b) <= MM_VMEM_BUDGET),
                  (row_blocks[-1], col_blocks[-1]))
    assert m % tm == 0 and n % tn == 0
    once = dict(pipeline_mode=pl.Buffered(1)) if staged else {}
    in_specs = [pl.BlockSpec((tm, k), lambda i, j: (i, 0), **once)]
    args = [x]
    if ms:
        in_specs.append(pl.BlockSpec((ms, k), lambda i, j: (0, 0), **once))
        args.append(side_x)
    if has_gain:
        in_specs.append(pl.BlockSpec((1, k), lambda i, j: (0, 0), **once))
        args.append(gain.reshape(1, k).astype(f32))
    for w in ws:
        if w_t:
            in_specs.append(pl.BlockSpec((None, tn, k), lambda i, j: (layer, j, 0)))
        elif layer is None:
            in_specs.append(pl.BlockSpec((k, tn), lambda i, j: (0, j)))
        else:
            in_specs.append(pl.BlockSpec((None, k, tn), lambda i, j: (layer, 0, j)))
        args.append(w)
    if tail:
        in_specs.append(pl.BlockSpec((None, tail, k), lambda i, j: (layer, n // tail, 0)))
        args.append(ws[0])
    if has_res:
        in_specs.append(pl.BlockSpec((tm, tn), lambda i, j: (i, j)))
        args.append(res)
        if ms:
            in_specs.append(pl.BlockSpec((ms, tn), lambda i, j: (0, j)))
            args.append(side_res)
    out_specs = [pl.BlockSpec((tm, tn), lambda i, j: (i, j))]
    out_shape = [jax.ShapeDtypeStruct((m, n), out_dtype)]
    if ms:
        out_specs.append(pl.BlockSpec((ms, tn), lambda i, j: (i, j)))
        out_shape.append(jax.ShapeDtypeStruct((m // tm * ms, n), out_dtype))
    if tail:
        out_specs.append(pl.BlockSpec((tm, tail), lambda i, j: (i, 0)))
        out_shape.append(jax.ShapeDtypeStruct((m, tail), f32))
        if ms:
            out_specs.append(pl.BlockSpec((ms, tail), lambda i, j: (i, 0)))
            out_shape.append(jax.ShapeDtypeStruct((m // tm * ms, tail), f32))
    body = functools.partial(_mm_body, n_w=n_w, has_gain=has_gain, has_res=has_res, epi=epi, tm=tm, ms=ms, w_t=w_t,
                             staged=staged, tail=tail)
    outs = pl.pallas_call(
        body,
        grid=(m // tm, n // tn),
        in_specs=in_specs,
        out_specs=out_specs,
        out_shape=out_shape,
        scratch_shapes=[pltpu.VMEM((tm + ms, k), bf16)] if staged else [],
        compiler_params=_cparams(2, vmem_estimate(tm, tn) + 6 * MIB),
        name="mm_" + epi,
    )(*args)
    outs = list(outs)
    if ms:
        outs[1] = outs[1][:ms]
        if tail:
            outs[3] = outs[3][:ms]
    return outs if len(outs) > 1 else outs[0]


def _rms_body(x_ref, g_ref, o_ref):
    x = x_ref[...]
    o_ref[...] = x * lax.rsqrt(jnp.mean(x * x, axis=-1, keepdims=True) + NORM_EPS) * g_ref[...]


def _rms_norm(x, gain):
    m, d = x.shape
    tm = min(m, 512)
    return pl.pallas_call(
        _rms_body,
        grid=(m // tm,),
        in_specs=[pl.BlockSpec((tm, d), lambda i: (i, 0)), pl.BlockSpec((1, d), lambda i: (0, 0))],
        out_specs=pl.BlockSpec((tm, d), lambda i: (i, 0)),
        out_shape=jax.ShapeDtypeStruct((m, d), f32),
        compiler_params=_cparams(1, 4 * tm * d * 4 + 4 * MIB),
        name="final_rms",
    )(x, gain.reshape(1, d))


def _ssd_body(z_ref, xp_ref, bc_ref, dtq_ref, dtt_ref, cw_ref, cb_ref, dtb_ref, dtbt_ref, alog_ref, alogt_ref,
              dsk_ref, gn_ref, h0_ref, c0_ref, yg_ref, ssm_ref, ht_ref, xs_ref, xc_ref, *, valid_len):
    q = SSD_CHUNK
    c = pl.program_id(1)
    n_c = pl.num_programs(1)

    @pl.when(c == 0)
    def _():
        ht_ref[...] = h0_ref[0]
        xs_ref[0:V7X_SUBLANES, :] = c0_ref[0]

    xs_ref[V7X_SUBLANES:V7X_SUBLANES + q, 0:SSD_INNER] = xp_ref[0]
    xs_ref[V7X_SUBLANES:V7X_SUBLANES + q, SSD_INNER:SSD_CONV_DIM] = bc_ref[0]
    lane_step = 512
    for s in range(SSD_CONV_DIM // lane_step):
        sl = slice(s * lane_step, (s + 1) * lane_step)
        acc = cb_ref[:, sl] + cw_ref[SSD_CONV - 1:SSD_CONV, sl] * xs_ref[V7X_SUBLANES:V7X_SUBLANES + q, sl]
        for kk in range(1, SSD_CONV):
            acc = acc + cw_ref[SSD_CONV - 1 - kk:SSD_CONV - kk, sl] * xs_ref[V7X_SUBLANES - kk:V7X_SUBLANES - kk + q, sl]
        xc_ref[:, sl] = jax.nn.silu(acc)
    xs_ref[0:V7X_SUBLANES, :] = xs_ref[q:q + V7X_SUBLANES, :]

    ii = lax.broadcasted_iota(i32, (q, q), 0)
    jj = lax.broadcasted_iota(i32, (q, q), 1)
    causal = ii >= jj
    tril = jnp.where(causal, 1.0, 0.0).astype(bf16)
    triu = jnp.where(jj >= ii, 1.0, 0.0).astype(bf16)

    n_terms = 3
    hrow = lax.broadcasted_iota(i32, (n_terms * SSD_HPG, SSD_GROUP_W), 0) % SSD_HPG
    spread_x = jnp.where(lax.broadcasted_iota(i32, hrow.shape, 1) // SSD_HEAD_DIM == hrow, 1.0, 0.0).astype(bf16)
    prow = lax.broadcasted_iota(i32, (n_terms * SSD_HPG, SSD_HPG * q), 0) % SSD_HPG
    spread_q = jnp.where(lax.broadcasted_iota(i32, prow.shape, 1) // q == prow, 1.0, 0.0).astype(bf16)
    low_half = lax.broadcasted_iota(i32, (q, V7X_LANES), 1) < SSD_HEAD_DIM

    def spread(cols, mat):
        return _dot(jnp.concatenate(_split3(cols), axis=1), mat)

    def group(g):
        dt = jax.nn.softplus(dtq_ref[0, g] + dtb_ref[g])
        dtt = jax.nn.softplus(dtt_ref[0, g] + dtbt_ref[g])
        if valid_len is not None:
            dt = jnp.where(lax.broadcasted_iota(i32, dt.shape, 0) < valid_len, dt, 0.0)
            dtt = jnp.where(lax.broadcasted_iota(i32, dtt.shape, 1) < valid_len, dtt, 0.0)
        adt = dt * (-jnp.exp(alog_ref[g]))
        adtt = dtt * (-jnp.exp(alogt_ref[g]))
        acum = sum(_dot(tril, p) for p in _split3(adt))
        acumt = sum(_dot(p, triu) for p in _split3(adtt))
        last = acum[q - 1:q, :]
        col_q = spread(acum, spread_q)
        e_x = spread(jnp.exp(acum), spread_x)
        w_x = spread(jnp.exp(last - acum) * dt, spread_x)

        xoff = g * SSD_GROUP_W
        boff = SSD_INNER + g * SSD_STATE
        coff = SSD_INNER + SSD_GROUPS * SSD_STATE + g * SSD_STATE
        xg = xc_ref[:, pl.ds(xoff, SSD_GROUP_W)]
        bg = xc_ref[:, pl.ds(boff, SSD_STATE)].astype(bf16)
        cg = xc_ref[:, pl.ds(coff, SSD_STATE)].astype(bf16)
        cbm = _dot_nt(cg, bg)
        htg = ht_ref[:, pl.ds(xoff, SSD_GROUP_W)]
        yoff = _dot(cg, htg.astype(bf16))

        yds = []
        for pr in range(SSD_HPG // 2):
            xp = xg[:, pr * V7X_LANES:(pr + 1) * V7X_LANES]
            halves = (jnp.where(low_half, xp, 0.0).astype(bf16), jnp.where(low_half, 0.0, xp).astype(bf16))
            acc = None
            for r, xh in zip((2 * pr, 2 * pr + 1), halves):
                seg = col_q[:, r * q:(r + 1) * q] - acumt[r:r + 1, :]
                dec = jnp.exp(jnp.where(causal, seg, -jnp.inf))
                part = _dot((cbm * dec * dtt[r:r + 1, :]).astype(bf16), xh)
                acc = part if acc is None else acc + part
            yds.append(acc)
        yd = jnp.concatenate(yds, axis=1)
        yg = yd + yoff * e_x + dsk_ref[:, pl.ds(xoff, SSD_GROUP_W)] * xg

        ht_ref[:, pl.ds(xoff, SSD_GROUP_W)] = htg * e_x[q - 1:q, :] + _dot_tn(bg, (xg * w_x).astype(bf16))

        gz = yg * jax.nn.silu(z_ref[0, :, pl.ds(xoff, SSD_GROUP_W)])
        gzn = gz * lax.rsqrt(jnp.mean(gz * gz, axis=-1, keepdims=True) + NORM_EPS)
        yg_ref[0, :, pl.ds(xoff, SSD_GROUP_W)] = (gzn * gn_ref[:, pl.ds(xoff, SSD_GROUP_W)]).astype(yg_ref.dtype)

    for g in range(SSD_GROUPS):
        group(g)

    @pl.when(c == n_c - 1)
    def _():
        for g in range(SSD_GROUPS):
            sl = slice(g * SSD_GROUP_W, (g + 1) * SSD_GROUP_W)
            ssm_ref[0, sl, :] = ht_ref[:, sl].T


def _ssd_scan(proj, dt_raw, conv0, ssm0, conv_w, conv_b, dt_bias, a_log, d_skip, gate_norm, *, valid_len=None):
    b_, l_, _ = proj.shape
    q = SSD_CHUNK
    assert l_ % q == 0
    n_c = l_ // q
    g_, r_ = SSD_GROUPS, SSD_HPG
    dtq = dt_raw.reshape(b_, l_, g_, r_).transpose(0, 2, 1, 3)
    dtt = dt_raw.reshape(b_, l_, g_, r_).transpose(0, 2, 3, 1)
    h0t = ssm0.reshape(b_, SSD_INNER, SSD_STATE).transpose(0, 2, 1)
    c0 = jnp.pad(conv0, ((0, 0), (V7X_SUBLANES - (SSD_CONV - 1), 0), (0, 0)))
    dtb = dt_bias.reshape(g_, 1, r_)
    dtbt = dt_bias.reshape(g_, r_, 1)
    alog = a_log.reshape(g_, 1, r_)
    alogt = a_log.reshape(g_, r_, 1)
    dsk = jnp.repeat(d_skip, SSD_HEAD_DIM).reshape(1, SSD_INNER)
    full = lambda shape: pl.BlockSpec(shape, lambda b, c: (0,) * len(shape))
    in_specs = [
        pl.BlockSpec((1, q, SSD_INNER), lambda b, c: (b, c, 0)),
        pl.BlockSpec((1, q, SSD_INNER), lambda b, c: (b, c, 1)),
        pl.BlockSpec((1, q, SSD_BC_W), lambda b, c: (b, c, 2 * SSD_INNER // SSD_BC_W)),
        pl.BlockSpec((1, g_, q, r_), lambda b, c: (b, 0, c, 0)),
        pl.BlockSpec((1, g_, r_, q), lambda b, c: (b, 0, 0, c)),
        full((SSD_CONV, SSD_CONV_DIM)), full((1, SSD_CONV_DIM)),
        full((g_, 1, r_)), full((g_, r_, 1)), full((g_, 1, r_)), full((g_, r_, 1)), full((1, SSD_INNER)),
        full((1, SSD_INNER)),
        pl.BlockSpec((1, SSD_STATE, SSD_INNER), lambda b, c: (b, 0, 0)),
        pl.BlockSpec((1, V7X_SUBLANES, SSD_CONV_DIM), lambda b, c: (b, 0, 0)),
    ]
    out_specs = [
        pl.BlockSpec((1, q, SSD_INNER), lambda b, c: (b, c, 0)),
        pl.BlockSpec((1, SSD_INNER, SSD_STATE), lambda b, c: (b, 0, 0)),
    ]
    yg, ssm = pl.pallas_call(
        functools.partial(_ssd_body, valid_len=valid_len),
        grid=(b_, n_c),
        in_specs=in_specs,
        out_specs=out_specs,
        out_shape=[jax.ShapeDtypeStruct((b_, l_, SSD_INNER), bf16),
                   jax.ShapeDtypeStruct((b_, SSD_INNER, SSD_STATE), f32)],
        scratch_shapes=[pltpu.VMEM((SSD_STATE, SSD_INNER), f32),
                        pltpu.VMEM((q + 2 * V7X_SUBLANES, SSD_CONV_DIM), f32),
                        pltpu.VMEM((q, SSD_CONV_DIM), f32)],
        compiler_params=_cparams(2, 48 * MIB),
        name="ssd_scan",
    )(proj, proj, proj, dtq, dtt, conv_w, conv_b.reshape(1, -1), dtb, dtbt, alog, alogt, dsk,
      gate_norm.reshape(1, -1), h0t, c0)
    return yg, ssm.reshape(b_, SSD_HEADS, SSD_HEAD_DIM, SSD_STATE)


def _trig_body(pos_ref, invf_ref, sgn_ref, cos_ref, sin_ref):
    ang = pos_ref[...].astype(f32) * invf_ref[...]
    rot = lax.broadcasted_iota(i32, ang.shape, 1) < ROPE_DIM
    cos_ref[...] = jnp.where(rot, jnp.cos(ang), 1.0)
    sin_ref[...] = jnp.sin(ang) * sgn_ref[...]


def _trig_tables(pos):
    n = pos.shape[0]
    inv_freq = ROPE_THETA ** (-jnp.arange(ROPE_HALF, dtype=f32) * 2.0 / ROPE_DIM)
    pad = jnp.zeros((MOBA_HEAD_DIM - ROPE_DIM,), f32)
    invf = jnp.concatenate([inv_freq, inv_freq, pad]).reshape(1, MOBA_HEAD_DIM)
    sgn = jnp.concatenate([-jnp.ones((ROPE_HALF,), f32), jnp.ones((ROPE_HALF,), f32), pad]).reshape(1, MOBA_HEAD_DIM)
    out = jax.ShapeDtypeStruct((n, MOBA_HEAD_DIM), f32)
    return pl.pallas_call(_trig_body, out_shape=[out, out], name="rope_tables")(pos.reshape(n, 1).astype(i32), invf, sgn)


def _rotate(x, cos_t, sin_t):
    lane = lax.broadcasted_iota(i32, x.shape, 1)
    partner = jnp.where(lane < ROPE_HALF, pltpu.roll(x, MOBA_HEAD_DIM - ROPE_HALF, 1), pltpu.roll(x, ROPE_HALF, 1))
    return x * cos_t + partner * sin_t


def _prep_body(qkv_ref, cos_ref, sin_ref, q_ref, k_ref, kb_ref, vb_ref, *maybe_kmean, with_kmean):
    cos_t = cos_ref[...]
    sin_t = sin_ref[...]
    for h in range(MOBA_HEADS):
        sl = slice(h * MOBA_HEAD_DIM, (h + 1) * MOBA_HEAD_DIM)
        q_ref[:, sl] = _rotate(qkv_ref[:, sl], cos_t, sin_t)
        kr = _rotate(qkv_ref[:, ATTN_W + h * MOBA_HEAD_DIM:ATTN_W + (h + 1) * MOBA_HEAD_DIM], cos_t, sin_t)
        k_ref[:, sl] = kr
        kb_ref[:, sl] = kr.astype(bf16)
        if with_kmean:
            maybe_kmean[0][0, :, sl] = jnp.sum(kr, axis=0, keepdims=True) * (1.0 / MOBA_BLOCK)
    vb_ref[...] = qkv_ref[:, 2 * ATTN_W:3 * ATTN_W].astype(bf16)


def _moba_prep(qkv, cos_t, sin_t, *, rows, with_kmean):
    m = qkv.shape[0]
    n_tab = cos_t.shape[0] // rows
    row_spec = lambda w: pl.BlockSpec((rows, w), lambda i: (i, 0))
    tab_spec = pl.BlockSpec((rows, MOBA_HEAD_DIM), lambda i: (i % n_tab, 0))
    out_shape = [jax.ShapeDtypeStruct((m, ATTN_W), f32), jax.ShapeDtypeStruct((m, ATTN_W), f32),
                 jax.ShapeDtypeStruct((m, ATTN_W), bf16), jax.ShapeDtypeStruct((m, ATTN_W), bf16)]
    out_specs = [row_spec(ATTN_W)] * 4
    if with_kmean:
        out_shape.append(jax.ShapeDtypeStruct((m // rows, 1, ATTN_W), f32))
        out_specs.append(pl.BlockSpec((1, 1, ATTN_W), lambda i: (i, 0, 0)))
    return pl.pallas_call(
        functools.partial(_prep_body, with_kmean=with_kmean),
        grid=(m // rows,),
        in_specs=[row_spec(3 * ATTN_W), tab_spec, tab_spec],
        out_specs=out_specs,
        out_shape=out_shape,
        compiler_params=_cparams(1, 40 * MIB),
        name="moba_prep",
    )(qkv, cos_t, sin_t)


def _gate_scores(q, km):
    qh, qm, _ = _split3(q)
    kh, kmid, _ = _split3(km)
    return _dot_nt(qh, kh) + _dot_nt(qh, kmid) + _dot_nt(qm, kh)


def _top_blocks(gate, n_take):
    lane = lax.broadcasted_iota(i32, gate.shape, 1)
    picks = []
    for _ in range(n_take):
        mx = jnp.max(gate, axis=-1, keepdims=True)
        idx = jnp.min(jnp.where(gate == mx, lane, gate.shape[1]), axis=-1, keepdims=True)
        picks.append(idx)
        gate = jnp.where(lane == idx, -jnp.inf, gate)
    return picks


ATTN_HEADS_PER_STEP = 4
ATTN_MASKED = -1e30


def _attn_prompt_body(q_ref, kb_ref, vb_ref, km_ref, o_ref, s_ref, qa_ref):
    blk = MOBA_BLOCK
    hd = MOBA_HEAD_DIM
    cur = pl.program_id(2)
    n_blk = km_ref.shape[1]
    n_all = n_blk * blk
    scale = hd ** -0.5
    blk_id = lax.broadcasted_iota(i32, (n_blk, blk), 0)
    past = blk_id < cur
    causal = lax.broadcasted_iota(i32, (blk, blk), 1) <= lax.broadcasted_iota(i32, (blk, blk), 0)
    ones = jnp.ones((blk, hd), bf16)
    lane = lax.broadcasted_iota(i32, (blk, hd), 1)
    eye = jnp.where(lax.broadcasted_iota(i32, (n_blk, hd), 1) == lax.broadcasted_iota(i32, (n_blk, hd), 0),
                    1.0, 0.0).astype(bf16)
    own = pl.multiple_of(cur * blk, blk)
    own_slot = slice(n_all, n_all + blk)

    for hh in range(ATTN_HEADS_PER_STEP):
        hs = slice(hh * hd, (hh + 1) * hd)
        q = q_ref[:, hs]
        gate = jnp.where(past, _gate_scores(km_ref[0, :, hs], q), -jnp.inf)
        sel = jnp.zeros((n_blk, blk), i32)
        for _ in range(min(MOBA_TOPK, n_blk)):
            mx = jnp.max(gate, axis=0, keepdims=True)
            idx = jnp.min(jnp.where(gate == mx, blk_id, n_blk), axis=0, keepdims=True)
            sel = jnp.where((blk_id == idx) & past, 1, sel)
            gate = jnp.where(blk_id == idx, -jnp.inf, gate)
        bias_t = jnp.where(sel > 0, 0.0, ATTN_MASKED).astype(bf16)
        qa_ref[hh, :, 0:hd] = (q * scale).astype(bf16)
        qa_ref[hh, :, hd:2 * hd] = _dot_tn(bias_t, eye).astype(bf16)

    def attend(n_past):
        for hh in range(ATTN_HEADS_PER_STEP):
            hs = slice(hh * hd, (hh + 1) * hd)
            q_aug = qa_ref[hh]
            s = jnp.where(causal, _dot_nt(q_aug[:, 0:hd], kb_ref[pl.ds(own, blk), hs]), -jnp.inf)
            s_ref[hh, :, own_slot] = s
            m2 = jnp.maximum(s[:, :hd], s[:, hd:])
            for n in range(n_past):
                keys = slice(n * blk, (n + 1) * blk)
                k_aug = jnp.concatenate([kb_ref[keys, hs], jnp.where(lane == n, 1.0, 0.0).astype(bf16)], axis=1)
                s = _dot_nt(q_aug, k_aug)
                s_ref[hh, :, keys] = s
                m2 = jnp.maximum(m2, jnp.maximum(s[:, :hd], s[:, hd:]))
            m = jnp.broadcast_to(jnp.max(m2, axis=-1, keepdims=True), (blk, blk))
            p = jnp.exp(s_ref[hh, :, own_slot] - m).astype(bf16)
            acc = _dot(p, jnp.concatenate([vb_ref[pl.ds(own, blk), hs], ones], axis=1))
            for n in range(n_past):
                keys = slice(n * blk, (n + 1) * blk)
                p = jnp.exp(s_ref[hh, :, keys] - m).astype(bf16)
                acc = acc + _dot(p, jnp.concatenate([vb_ref[keys, hs], ones], axis=1))
            o_ref[:, hs] = (acc[:, :hd] / acc[:, hd:]).astype(o_ref.dtype)

    for pair in range((n_blk + 1) // 2):
        pl.when(cur // 2 == pair)(functools.partial(attend, min(2 * pair + 1, n_blk - 1)))


def _moba_attn_prompt(q, kb, vb, kmean, b_, l_):
    n_blk = l_ // MOBA_BLOCK
    w = ATTN_HEADS_PER_STEP * MOBA_HEAD_DIM
    return pl.pallas_call(
        _attn_prompt_body,
        grid=(b_, MOBA_HEADS // ATTN_HEADS_PER_STEP, n_blk),
        in_specs=[
            pl.BlockSpec((MOBA_BLOCK, w), lambda b, h, t: (b * n_blk + t, h)),
            pl.BlockSpec((l_, w), lambda b, h, t: (b, h)),
            pl.BlockSpec((l_, w), lambda b, h, t: (b, h)),
            pl.BlockSpec((1, n_blk, w), lambda b, h, t: (b, 0, h)),
        ],
        out_specs=pl.BlockSpec((MOBA_BLOCK, w), lambda b, h, t: (b * n_blk + t, h)),
        out_shape=jax.ShapeDtypeStruct((b_ * l_, ATTN_W), bf16),
        scratch_shapes=[pltpu.VMEM((ATTN_HEADS_PER_STEP, MOBA_BLOCK, l_ + MOBA_BLOCK), f32),
                        pltpu.VMEM((ATTN_HEADS_PER_STEP, MOBA_BLOCK, 2 * MOBA_HEAD_DIM), bf16)],
        compiler_params=_cparams(3, 32 * MIB),
        name="moba_attn_prompt",
    )(q, kb, vb, kmean)


PAGES_PER_BLOCK = MOBA_BLOCK // PAGE_SIZE
KMEAN_BLOCKS_PER_STEP = 8
KMEAN_PAGES_PER_STEP = KMEAN_BLOCKS_PER_STEP * PAGES_PER_BLOCK


def _kmean_body(pt_ref, *refs):
    page_refs, o_ref = refs[:-1], refs[-1]
    for blk in range(KMEAN_BLOCKS_PER_STEP):
        tot = None
        for p in range(PAGES_PER_BLOCK):
            part = jnp.sum(page_refs[blk * PAGES_PER_BLOCK + p][...], axis=0)
            tot = part if tot is None else tot + part
        tot = tot * (1.0 / MOBA_BLOCK)
        for h in range(MOBA_HEADS):
            o_ref[0, blk:blk + 1, h * MOBA_HEAD_DIM:(h + 1) * MOBA_HEAD_DIM] = tot[h:h + 1, :]


def _cache_kmean(cache, layer, page_table):
    b_, n_pages = page_table.shape
    n_steps = n_pages // KMEAN_PAGES_PER_STEP
    pt_flat = page_table.reshape(-1)

    def page_spec(p):
        return pl.BlockSpec((None, None, PAGE_SIZE, MOBA_HEADS, MOBA_HEAD_DIM),
                            lambda b, s, pt: (layer, pt[b * n_pages + s * KMEAN_PAGES_PER_STEP + p], 0, 0, 0))

    return pl.pallas_call(
        _kmean_body,
        grid_spec=pltpu.PrefetchScalarGridSpec(
            num_scalar_prefetch=1,
            grid=(b_, n_steps),
            in_specs=[page_spec(p) for p in range(KMEAN_PAGES_PER_STEP)],
            out_specs=pl.BlockSpec((1, KMEAN_BLOCKS_PER_STEP, ATTN_W), lambda b, s, pt: (b, s, 0)),
        ),
        out_shape=jax.ShapeDtypeStruct((b_, n_pages // PAGES_PER_BLOCK, ATTN_W), f32),
        compiler_params=_cparams(2, 40 * MIB),
        name="cache_kmean",
    )(pt_flat, *([cache] * KMEAN_PAGES_PER_STEP))


def _sample_topk_body(q_ref, km_ref, o_ref, g_ref):
    rows = q_ref.shape[1]
    for h in range(MOBA_HEADS):
        sl = slice(h * MOBA_HEAD_DIM, (h + 1) * MOBA_HEAD_DIM)
        g_ref[h * rows:(h + 1) * rows, :] = _gate_scores(q_ref[0, :, sl], km_ref[0, :, sl])
    picks = _top_blocks(g_ref[...], MOBA_TOPK)
    lane = lax.broadcasted_iota(i32, o_ref.shape[1:], 1)
    out = jnp.zeros(o_ref.shape[1:], i32)
    for s, idx in enumerate(picks):
        out = jnp.where(lane == s, idx, out)
    o_ref[0] = out


def _sample_topk(q8, kmean):
    b_, rows, _ = q8.shape
    n_blk = kmean.shape[1]
    return pl.pallas_call(
        _sample_topk_body,
        grid=(b_,),
        in_specs=[pl.BlockSpec((1, rows, ATTN_W), lambda b: (b, 0, 0)),
                  pl.BlockSpec((1, n_blk, ATTN_W), lambda b: (b, 0, 0))],
        out_specs=pl.BlockSpec((1, MOBA_HEADS * rows, V7X_LANES), lambda b: (b, 0, 0)),
        out_shape=jax.ShapeDtypeStruct((b_, MOBA_HEADS * rows, V7X_LANES), i32),
        scratch_shapes=[pltpu.VMEM((MOBA_HEADS * rows, n_blk), f32)],
        compiler_params=_cparams(1, 16 * MIB),
        name="sample_topk",
    )(q8, kmean)


def _attn_sample_body(sel_ref, pt_ref, q_ref, kn_ref, vn_ref, ck_hbm, cv_hbm, o_ref, kbuf, vbuf, sems,
                      *, layer, n_tok, n_pages):
    pages_per_tok = MOBA_TOPK * PAGES_PER_BLOCK
    keys_per_tok = pages_per_tok * PAGE_SIZE
    n_heads = pl.num_programs(1)
    step = pl.program_id(0) * n_heads + pl.program_id(1)
    n_steps = pl.num_programs(0) * n_heads
    slot = lax.rem(step, 2)

    def page_copies(step_, slot_, known_source):
        b_ = step_ // n_heads
        h_ = lax.rem(step_, n_heads) if known_source else 0
        out = []
        for t in range(n_tok):
            for sp in range(pages_per_tok):
                page = 0
                if known_source:
                    blk = sel_ref[step_ * (n_tok * MOBA_TOPK) + t * MOBA_TOPK + sp // PAGES_PER_BLOCK]
                    page = pt_ref[b_ * n_pages + blk * PAGES_PER_BLOCK + sp % PAGES_PER_BLOCK]
                dst = (slot_, t, pl.ds(sp * PAGE_SIZE, PAGE_SIZE))
                out.append(pltpu.make_async_copy(ck_hbm.at[layer, page, :, h_, :], kbuf.at[dst], sems.at[0, slot_]))
                out.append(pltpu.make_async_copy(cv_hbm.at[layer, page, :, h_, :], vbuf.at[dst], sems.at[1, slot_]))
        return out

    @pl.when(step == 0)
    def _():
        for cp in page_copies(step, slot, True):
            cp.start()

    @pl.when(step + 1 < n_steps)
    def _():
        for cp in page_copies(step + 1, 1 - slot, True):
            cp.start()

    for cp in page_copies(step, slot, False):
        cp.wait()

    rows = q_ref.shape[1]
    scale = MOBA_HEAD_DIM ** -0.5
    qb = q_ref[0].astype(bf16)
    row = lax.broadcasted_iota(i32, (rows, keys_per_tok), 0)
    k_refs = [kbuf.at[slot, t] for t in range(n_tok)]
    v_refs = [vbuf.at[slot, t] for t in range(n_tok)]
    scores = []
    for t in range(n_tok):
        s = _dot_nt(qb, k_refs[t][...].astype(bf16)) * scale
        scores.append(jnp.where(row == t, s, -jnp.inf))
    r8 = lax.broadcasted_iota(i32, (rows, rows), 0)
    c8 = lax.broadcasted_iota(i32, (rows, rows), 1)
    s_own = _dot_nt(qb, kn_ref[0]) * scale
    s_own = jnp.where((c8 <= r8) & (c8 < n_tok), s_own, -jnp.inf)
    m = jnp.max(s_own, axis=-1, keepdims=True)
    for s in scores:
        m = jnp.maximum(m, jnp.max(s, axis=-1, keepdims=True))
    m = jnp.where(m == -jnp.inf, 0.0, m)
    p_own = jnp.exp(s_own - m)
    l = jnp.sum(p_own, axis=-1, keepdims=True)
    acc = _dot(p_own.astype(bf16), vn_ref[0])
    for t in range(n_tok):
        p = jnp.exp(scores[t] - m)
        l = l + jnp.sum(p, axis=-1, keepdims=True)
        acc = acc + _dot(p.astype(bf16), v_refs[t][...].astype(bf16))
    l = jnp.where(l == 0.0, 1.0, l)
    o_ref[0] = (acc / l).astype(o_ref.dtype)


def _moba_attn_sample(sel, page_table, q8, kn8, vn8, cache_k, cache_v, layer, n_tok):
    b_, rows, _ = q8.shape
    n_pages = page_table.shape[1]
    sel_flat = sel.reshape(-1).astype(i32)
    pt_flat = page_table.reshape(-1).astype(i32)
    head_spec = pl.BlockSpec((1, rows, MOBA_HEAD_DIM), lambda b, h, sel_r, pt_r: (b, 0, h))
    page_buf = pltpu.VMEM((2, n_tok, MOBA_TOPK * MOBA_BLOCK, MOBA_HEAD_DIM), f32)
    return pl.pallas_call(
        functools.partial(_attn_sample_body, layer=layer, n_tok=n_tok, n_pages=n_pages),
        grid_spec=pltpu.PrefetchScalarGridSpec(
            num_scalar_prefetch=2,
            grid=(b_, MOBA_HEADS),
            in_specs=[head_spec, head_spec, head_spec,
                      pl.BlockSpec(memory_space=pl.ANY), pl.BlockSpec(memory_space=pl.ANY)],
            out_specs=head_spec,
            scratch_shapes=[page_buf, page_buf, pltpu.SemaphoreType.DMA((2, 2))],
        ),
        out_shape=jax.ShapeDtypeStruct((b_, rows, ATTN_W), bf16),
        compiler_params=_cparams(2, 24 * MIB),
        name="moba_attn_sample",
    )(sel_flat, pt_flat, q8, kn8, vn8, cache_k, cache_v)


def _gmlp_body(uv_ref, lg_ref, lb_ref, ws_ref, bst_ref, o_ref, *maybe_v, with_v):
    n = GMLP_CHUNK
    vf = uv_ref[:, GMLP_DIM:2 * GMLP_DIM]
    mu = jnp.mean(vf, axis=-1, keepdims=True)
    var = jnp.mean(jnp.square(vf - mu), axis=-1, keepdims=True)
    v = (vf - mu) * lax.rsqrt(var + NORM_EPS) * lg_ref[...] + lb_ref[...]
    if with_v:
        maybe_v[0][...] = v
    ii = lax.broadcasted_iota(i32, (n, n), 0)
    jj = lax.broadcasted_iota(i32, (n, n), 1)
    for g in range(GMLP_GROUPS):
        sl = slice(g * GMLP_GROUP_DIM, (g + 1) * GMLP_GROUP_DIM)
        w = jnp.where(jj <= ii, ws_ref[g], 0.0).astype(bf16)
        s = _dot(w, v[:, sl].astype(bf16)) + bst_ref[:, g:g + 1]
        o_ref[:, sl] = (uv_ref[:, sl] * s).astype(o_ref.dtype)


def _gmlp_gate(uv, ln_g, ln_b, w_s, b_s, *, with_v):
    m = uv.shape[0]
    n = GMLP_CHUNK
    out_shape = [jax.ShapeDtypeStruct((m, GMLP_DIM), bf16)]
    out_specs = [pl.BlockSpec((n, GMLP_DIM), lambda i: (i, 0))]
    if with_v:
        out_shape.append(jax.ShapeDtypeStruct((m, GMLP_DIM), f32))
        out_specs.append(pl.BlockSpec((n, GMLP_DIM), lambda i: (i, 0)))
    return pl.pallas_call(
        functools.partial(_gmlp_body, with_v=with_v),
        grid=(m // n,),
        in_specs=[pl.BlockSpec((n, 2 * GMLP_DIM), lambda i: (i, 0)),
                  pl.BlockSpec((1, GMLP_DIM), lambda i: (0, 0)),
                  pl.BlockSpec((1, GMLP_DIM), lambda i: (0, 0)),
                  pl.BlockSpec((GMLP_GROUPS, n, n), lambda i: (0, 0, 0)),
                  pl.BlockSpec((n, GMLP_GROUPS), lambda i: (0, 0))],
        out_specs=out_specs,
        out_shape=out_shape,
        compiler_params=_cparams(1, 24 * MIB),
        name="gmlp_gate",
    )(uv, ln_g.reshape(1, -1), ln_b.reshape(1, -1), w_s, b_s.T)


def _pad_rows(x, rows):
    return jnp.pad(x, ((0, 0), (0, rows - x.shape[1]), (0, 0)))


def _ssd_group(proj, dt_raw, b_, l_, conv0, ssm0, scan_w):
    proj3 = proj.reshape(b_, l_, SSD_MAIN_W)
    dt3 = dt_raw.reshape(b_, l_, SSD_HEADS)
    if l_ < SSD_CONV - 1:
        new_conv = jnp.concatenate([conv0, proj3[:, :, SSD_INNER:]], axis=1)[:, -(SSD_CONV - 1):]
    else:
        new_conv = proj3[:, l_ - (SSD_CONV - 1):, SSD_INNER:]
    lp = -(-l_ // SSD_CHUNK) * SSD_CHUNK
    valid_len = None
    if lp != l_:
        assert lp == SSD_CHUNK
        proj3 = _pad_rows(proj3, lp)
        dt3 = _pad_rows(dt3, lp)
        valid_len = l_
    yg, ssm = _ssd_scan(proj3, dt3, conv0, ssm0, *scan_w, valid_len=valid_len)
    if lp != l_:
        yg = yg[:, :l_]
    return yg.reshape(b_ * l_, SSD_INNER), new_conv, ssm


def _ssd_layer(hp, hs, bp, lp_, bs, ls, gain, conv0_p, ssm0_p, conv0_s, ssm0_s, j, w_in_t, scan_w, w_out):
    proj_p, proj_s, dt_p, dt_s = _mm(hp, [w_in_t], layer=j, w_t=True, n_out=SSD_MAIN_W, tail=SSD_HEADS, gain=gain,
                                     side_x=hs)
    yg_p, conv_p, ssm_p = _ssd_group(proj_p, dt_p, bp, lp_, conv0_p, ssm0_p, scan_w)
    yg_s, conv_s, ssm_s = _ssd_group(proj_s, dt_s, bs, ls, conv0_s, ssm0_s, scan_w)
    hp, hs = _mm(yg_p, [w_out], layer=j, res=hp, side_x=yg_s, side_res=hs)
    return hp, hs, (conv_p, ssm_p), (conv_s, ssm_s)


def _ffn_layer(hp, hs, i, gain, w_gate, w_up, w_down):
    tp, ts = _mm(hp, [w_gate, w_up], layer=i, gain=gain, epi="swiglu", out_dtype=bf16, side_x=hs)
    return _mm(tp, [w_down], layer=i, res=hp, side_x=ts, side_res=hs)


def kernel(x_prompt, x_sample, state_ssm, state_conv, cache_k, cache_v, page_table, norm_mix, norm_ffn, norm_final, ssd_w_in, ssd_conv_w, ssd_conv_b, ssd_dt_bias, ssd_a_log, ssd_d, ssd_gate_norm, ssd_w_out, moba_w_qkv, moba_w_o, gmlp_w_in, gmlp_ln_g, gmlp_ln_b, gmlp_w_s, gmlp_b_s, gmlp_w_out, ffn_w_gate, ffn_w_up, ffn_w_down):
    bp, lp_, d = x_prompt.shape
    bs, ls, _ = x_sample.shape
    hp = x_prompt.reshape(bp * lp_, d)
    hs = x_sample.reshape(bs * ls, d)
    n_pages = page_table.shape[1]
    past = n_pages * PAGE_SIZE
    tok_rows = V7X_SUBLANES
    assert ls <= tok_rows and past % MOBA_BLOCK == 0 and lp_ % MOBA_BLOCK == 0

    ssm_p, ssm_s, conv_p, conv_s = [], [], [], []
    k_p, v_p, k_s, v_s, gv_s = [], [], [], [], []
    for i in range(DEPTH):
        kind = i % N_MIXERS
        j = i // N_MIXERS
        if kind == 0:
            w_in_t = jnp.swapaxes(ssd_w_in, 1, 2)
            scan_w = (ssd_conv_w[j], ssd_conv_b[j], ssd_dt_bias[j], ssd_a_log[j], ssd_d[j], ssd_gate_norm[j])
            zero_conv = jnp.zeros((bp, SSD_CONV - 1, SSD_CONV_DIM), f32)
            zero_ssm = jnp.zeros((bp, SSD_HEADS, SSD_HEAD_DIM, SSD_STATE), f32)
            hp, hs, (cp_, sp_), (cs_, ss_) = _ssd_layer(hp, hs, bp, lp_, bs, ls, norm_mix[i], zero_conv, zero_ssm,
                                                        state_conv[j], state_ssm[j], j, w_in_t, scan_w, ssd_w_out)
            conv_p.append(cp_)
            ssm_p.append(sp_)
            conv_s.append(cs_)
            ssm_s.append(ss_)
        elif kind == 1:
            qkv, qkv_s = _mm(hp, [moba_w_qkv], layer=j, gain=norm_mix[i], side_x=hs)
            cos_t, sin_t = _trig_tables(jnp.arange(lp_, dtype=i32))
            q, k, kb, vb, kmean = _moba_prep(qkv, cos_t, sin_t, rows=MOBA_BLOCK, with_kmean=True)
            o = _moba_attn_prompt(q, kb, vb, kmean.reshape(bp, lp_ // MOBA_BLOCK, ATTN_W), bp, lp_)
            k_p.append(k.reshape(bp, lp_, MOBA_HEADS, MOBA_HEAD_DIM))
            v_p.append(qkv[:, 2 * ATTN_W:].reshape(bp, lp_, MOBA_HEADS, MOBA_HEAD_DIM))
            pos_s = past + (jnp.arange(bs * ls, dtype=i32) % ls)
            cos_s, sin_s = _trig_tables(pos_s)
            q_s, kn, knb, vnb = _moba_prep(qkv_s, cos_s, sin_s, rows=bs * ls, with_kmean=False)
            kmean_c = _cache_kmean(cache_k, j, page_table)
            q8 = _pad_rows(q_s.reshape(bs, ls, ATTN_W), tok_rows)
            kn8 = _pad_rows(knb.reshape(bs, ls, ATTN_W), tok_rows)
            vn8 = _pad_rows(vnb.reshape(bs, ls, ATTN_W), tok_rows)
            top = _sample_topk(q8, kmean_c)
            sel = top.reshape(bs, MOBA_HEADS, tok_rows, V7X_LANES)[:, :, :ls, :MOBA_TOPK]
            o_s = _moba_attn_sample(sel, page_table, q8, kn8, vn8, cache_k, cache_v, j, ls)
            k_s.append(kn.reshape(bs, ls, MOBA_HEADS, MOBA_HEAD_DIM))
            v_s.append(qkv_s[:, 2 * ATTN_W:].reshape(bs, ls, MOBA_HEADS, MOBA_HEAD_DIM))
            hp, hs = _mm(o, [moba_w_o], layer=j, res=hp, side_x=o_s[:, :ls].reshape(bs * ls, ATTN_W), side_res=hs)
        else:
            w = (gmlp_ln_g[j], gmlp_ln_b[j], gmlp_w_s[j], gmlp_b_s[j])
            uv, uv_s = _mm(hp, [gmlp_w_in], layer=j, gain=norm_mix[i], epi="gelu", side_x=hs)
            (gated,) = _gmlp_gate(uv, *w, with_v=False)
            uv_s = _pad_rows(uv_s.reshape(bs, ls, 2 * GMLP_DIM), GMLP_CHUNK).reshape(bs * GMLP_CHUNK, 2 * GMLP_DIM)
            gated_s, v_rows = _gmlp_gate(uv_s, *w, with_v=True)
            gated_s = gated_s.reshape(bs, GMLP_CHUNK, GMLP_DIM)[:, :ls].reshape(bs * ls, GMLP_DIM)
            gv_s.append(v_rows.reshape(bs, GMLP_CHUNK, GMLP_DIM)[:, :ls])
            hp, hs = _mm(gated, [gmlp_w_out], layer=j, res=hp, side_x=gated_s, side_res=hs)
        hp, hs = _ffn_layer(hp, hs, i, norm_ffn[i], ffn_w_gate, ffn_w_up, ffn_w_down)
    y_prompt = _rms_norm(hp, norm_final).reshape(bp, lp_, d)
    y_sample = _rms_norm(hs, norm_final).reshape(bs, ls, d)
    return (y_prompt, y_sample, jnp.stack(ssm_p), jnp.stack(ssm_s), jnp.stack(conv_p), jnp.stack(conv_s),
            jnp.stack(k_p), jnp.stack(v_p), jnp.stack(k_s), jnp.stack(v_s), jnp.stack(gv_s))
```

```python
import functools
import math

import jax
import jax.numpy as jnp
from jax import lax
from jax.experimental import pallas as pl
from jax.experimental.pallas import tpu as pltpu

f32 = jnp.float32
bf16 = jnp.bfloat16
i32 = jnp.int32

D_MODEL = 2048
DEPTH = 4
PAGE_SIZE = 128
N_MIXERS = 3
NORM_EPS = 1e-6
SSD_INNER = 4096
SSD_HEAD_DIM = 64
SSD_HEADS = 64
SSD_GROUPS = 8
SSD_HPG = 8
SSD_STATE = 128
SSD_CONV = 4
SSD_CHUNK = 128
SSD_GROUP_W = SSD_HPG * SSD_HEAD_DIM
SSD_BC_W = 2 * SSD_GROUPS * SSD_STATE
SSD_CONV_DIM = SSD_INNER + SSD_BC_W
SSD_MAIN_W = SSD_INNER + SSD_CONV_DIM
MOBA_HEADS = 16
MOBA_HEAD_DIM = 128
MOBA_BLOCK = 256
MOBA_TOPK = 3
ROPE_THETA = 500000.0
ROPE_DIM = 32
ROPE_HALF = ROPE_DIM // 2
ATTN_W = MOBA_HEADS * MOBA_HEAD_DIM
GMLP_DIM = 2048
GMLP_GROUPS = 16
GMLP_GROUP_DIM = 128
GMLP_CHUNK = 128

V7X_LANES = 128
V7X_SUBLANES = 8
V7X_VMEM_LIMIT_BYTES = 60 * 1024 * 1024
MIB = 1024 * 1024


def _cparams(n_grid, vmem_bytes):
    return pltpu.CompilerParams(
        dimension_semantics=("arbitrary",) * n_grid,
        vmem_limit_bytes=int(min(max(vmem_bytes, 16 * MIB), V7X_VMEM_LIMIT_BYTES)),
    )


def _split3(x):
    hi = x.astype(bf16)
    r1 = x - hi.astype(f32)
    mid = r1.astype(bf16)
    lo = (r1 - mid.astype(f32)).astype(bf16)
    return hi, mid, lo


def _dot(a, b):
    return jnp.dot(a, b, preferred_element_type=f32)


def _dot_nt(a, b):
    return lax.dot_general(a, b, (((1,), (1,)), ((), ())), preferred_element_type=f32)


def _dot_tn(a, b):
    return lax.dot_general(a, b, (((0,), (0,)), ((), ())), preferred_element_type=f32)


def _gelu_exact(x):
    return 0.5 * x * (1.0 + lax.erf(x * (1.0 / math.sqrt(2.0))))


MM_MAX_ROWS = 2048
MM_VMEM_BUDGET = 50 * MIB


def _mm_body(*refs, n_w, has_gain, has_res, epi, tm, ms, w_t, staged, tail):
    it = iter(refs)
    x_ref = next(it)
    sx_ref = next(it) if ms else None
    g_ref = next(it) if has_gain else None
    w_refs = [next(it) for _ in range(n_w)]
    wt_ref = next(it) if tail else None
    r_ref = next(it) if has_res else None
    sr_ref = next(it) if (has_res and ms) else None
    o_ref = next(it)
    so_ref = next(it) if ms else None
    ot_ref = next(it) if tail else None
    sot_ref = next(it) if (tail and ms) else None
    xs_ref = next(it) if staged else None
    i = pl.program_id(0)
    j = pl.program_id(1)
    mxu = _dot_nt if w_t else _dot

    if staged:
        def prep(v):
            v = v.astype(f32)
            if has_gain:
                v = v * lax.rsqrt(jnp.mean(v * v, axis=-1, keepdims=True) + NORM_EPS) * g_ref[...]
            return v.astype(bf16)

        @pl.when(j == 0)
        def _():
            xs_ref[0:tm, :] = prep(x_ref[...])

        if ms:
            @pl.when((j == 0) & (i == 0))
            def _():
                xs_ref[tm:tm + ms, :] = prep(sx_ref[...])

    wb = [w[...].astype(bf16) for w in w_refs]

    def product(xb):
        accs = [mxu(xb, w) for w in wb]
        if epi == "swiglu":
            return jax.nn.silu(accs[0]) * accs[1]
        if epi == "gelu":
            return _gelu_exact(accs[0])
        return accs[0]

    def store(main, side):
        if has_res:
            main = r_ref[...] + main
        o_ref[...] = main.astype(o_ref.dtype)
        if side is not None:
            if has_res:
                side = sr_ref[...] + side
            so_ref[...] = side.astype(so_ref.dtype)

    def run(with_side):
        if staged:
            rows = tm + ms if with_side else tm
            out = product(xs_ref[0:rows, :])
            store(out[0:tm], out[tm:rows] if with_side else None)
            if tail:
                @pl.when(j == 0)
                def _():
                    t = _dot_nt(xs_ref[0:rows, :], wt_ref[...].astype(bf16))
                    ot_ref[...] = t[0:tm]
                    if with_side:
                        sot_ref[...] = t[tm:rows]
        else:
            store(product(x_ref[...]), product(sx_ref[...]) if with_side else None)

    if ms:
        pl.when(i == 0)(functools.partial(run, True))

        @pl.when(i > 0)
        def _():
            run(False)
            so_ref[...] = jnp.zeros(so_ref.shape, so_ref.dtype)
            if tail:
                @pl.when(j == 0)
                def _():
                    sot_ref[...] = jnp.zeros(sot_ref.shape, sot_ref.dtype)
    else:
        run(False)


def _mm(x, ws, *, layer=None, w_t=False, n_out=None, tail=0, gain=None, res=None, side_x=None, side_res=None,
        epi="none", out_dtype=f32, max_rows=MM_MAX_ROWS):
    m, k = x.shape
    n_total = ws[0].shape[-2] if w_t else ws[0].shape[-1]
    n = n_out if n_out is not None else n_total
    ms = 0 if side_x is None else side_x.shape[0]
    n_w = len(ws)
    has_gain = gain is not None
    has_res = res is not None
    staged = has_gain or x.dtype != bf16
    assert not tail or (w_t and staged and n % tail == 0)
    out_b = jnp.dtype(out_dtype).itemsize

    def vmem_estimate(tm_, tn_, x_bufs_):
        return (x_bufs_ * tm_ * k * x.dtype.itemsize + (staged * (tm_ + ms) * k * 2) + n_w * k * tn_ * (2 * 4 + 2)
                + tm_ * tn_ * (2 * out_b + (8 if has_res else 0)) + n_w * (tm_ + ms) * tn_ * 4 + 2 * MIB)

    row_blocks = [t for t in (2048, 1024, 512, 256, 128, 64, 32, 16, 8) if t <= max_rows and m % t == 0] or [m]
    col_blocks = [t for t in (512, 256) if n % t == 0] or [n]
    tm, tn, x_bufs = next(((a, b, c) for a in row_blocks for b in col_blocks for c in ((2, 1) if staged else (2,))
                           if vmem_estimate(a, b, c) <= MM_VMEM_BUDGET), (row_blocks[-1], col_blocks[-1], 1))
    assert m % tm == 0 and n % tn == 0
    once = dict(pipeline_mode=pl.Buffered(1)) if x_bufs == 1 else {}
    in_specs = [pl.BlockSpec((tm, k), lambda i, j: (i, 0), **once)]
    args = [x]
    if ms:
        in_specs.append(pl.BlockSpec((ms, k), lambda i, j: (0, 0), **once))
        args.append(side_x)
    if has_gain:
        in_specs.append(pl.BlockSpec((1, k), lambda i, j: (0, 0), **once))
        args.append(gain.reshape(1, k).astype(f32))
    for w in ws:
        if w_t:
            in_specs.append(pl.BlockSpec((None, tn, k), lambda i, j: (layer, j, 0)))
        elif layer is None:
            in_specs.append(pl.BlockSpec((k, tn), lambda i, j: (0, j)))
        else:
            in_specs.append(pl.BlockSpec((None, k, tn), lambda i, j: (layer, 0, j)))
        args.append(w)
    if tail:
        in_specs.append(pl.BlockSpec((None, tail, k), lambda i, j: (layer, n // tail, 0)))
        args.append(ws[0])
    if has_res:
        in_specs.append(pl.BlockSpec((tm, tn), lambda i, j: (i, j)))
        args.append(res)
        if ms:
            in_specs.append(pl.BlockSpec((ms, tn), lambda i, j: (0, j)))
            args.append(side_res)
    out_specs = [pl.BlockSpec((tm, tn), lambda i, j: (i, j))]
    out_shape = [jax.ShapeDtypeStruct((m, n), out_dtype)]
    if ms:
        out_specs.append(pl.BlockSpec((ms, tn), lambda i, j: (i, j)))
        out_shape.append(jax.ShapeDtypeStruct((m // tm * ms, n), out_dtype))
    if tail:
        out_specs.append(pl.BlockSpec((tm, tail), lambda i, j: (i, 0)))
        out_shape.append(jax.ShapeDtypeStruct((m, tail), f32))
        if ms:
            out_specs.append(pl.BlockSpec((ms, tail), lambda i, j: (i, 0)))
            out_shape.append(jax.ShapeDtypeStruct((m // tm * ms, tail), f32))
    body = functools.partial(_mm_body, n_w=n_w, has_gain=has_gain, has_res=has_res, epi=epi, tm=tm, ms=ms, w_t=w_t,
                             staged=staged, tail=tail)
    outs = pl.pallas_call(
        body,
        grid=(m // tm, n // tn),
        in_specs=in_specs,
        out_specs=out_specs,
        out_shape=out_shape,
        scratch_shapes=[pltpu.VMEM((tm + ms, k), bf16)] if staged else [],
        compiler_params=_cparams(2, vmem_estimate(tm, tn, x_bufs) + 6 * MIB),
        name="mm_" + epi,
    )(*args)
    outs = list(outs)
    if ms:
        outs[1] = outs[1][:ms]
        if tail:
            outs[3] = outs[3][:ms]
    return outs if len(outs) > 1 else outs[0]


def _rms_body(x_ref, g_ref, o_ref):
    x = x_ref[...]
    o_ref[...] = x * lax.rsqrt(jnp.mean(x * x, axis=-1, keepdims=True) + NORM_EPS) * g_ref[...]


def _rms_norm(x, gain):
    m, d = x.shape
    tm = min(m, 512)
    return pl.pallas_call(
        _rms_body,
        grid=(m // tm,),
        in_specs=[pl.BlockSpec((tm, d), lambda i: (i, 0)), pl.BlockSpec((1, d), lambda i: (0, 0))],
        out_specs=pl.BlockSpec((tm, d), lambda i: (i, 0)),
        out_shape=jax.ShapeDtypeStruct((m, d), f32),
        compiler_params=_cparams(1, 4 * tm * d * 4 + 4 * MIB),
        name="final_rms",
    )(x, gain.reshape(1, d))


def _ssd_body(z_ref, xp_ref, bc_ref, dtq_ref, dtt_ref, cw_ref, cb_ref, dtb_ref, dtbt_ref, alog_ref, alogt_ref,
              dsk_ref, gn_ref, h0_ref, c0_ref, yg_ref, ssm_ref, ht_ref, xs_ref, xc_ref, *, valid_len):
    q = SSD_CHUNK
    c = pl.program_id(1)
    n_c = pl.num_programs(1)

    @pl.when(c == 0)
    def _():
        ht_ref[...] = h0_ref[0]
        xs_ref[0:V7X_SUBLANES, :] = c0_ref[0]

    xs_ref[V7X_SUBLANES:V7X_SUBLANES + q, 0:SSD_INNER] = xp_ref[0]
    xs_ref[V7X_SUBLANES:V7X_SUBLANES + q, SSD_INNER:SSD_CONV_DIM] = bc_ref[0]
    lane_step = 512
    for s in range(SSD_CONV_DIM // lane_step):
        sl = slice(s * lane_step, (s + 1) * lane_step)
        acc = cb_ref[:, sl] + cw_ref[SSD_CONV - 1:SSD_CONV, sl] * xs_ref[V7X_SUBLANES:V7X_SUBLANES + q, sl]
        for kk in range(1, SSD_CONV):
            acc = acc + cw_ref[SSD_CONV - 1 - kk:SSD_CONV - kk, sl] * xs_ref[V7X_SUBLANES - kk:V7X_SUBLANES - kk + q, sl]
        xc_ref[:, sl] = jax.nn.silu(acc)
    xs_ref[0:V7X_SUBLANES, :] = xs_ref[q:q + V7X_SUBLANES, :]

    ii = lax.broadcasted_iota(i32, (q, q), 0)
    jj = lax.broadcasted_iota(i32, (q, q), 1)
    causal = ii >= jj
    tril = jnp.where(causal, 1.0, 0.0).astype(bf16)
    triu = jnp.where(jj >= ii, 1.0, 0.0).astype(bf16)

    n_terms = 3
    hrow = lax.broadcasted_iota(i32, (n_terms * SSD_HPG, SSD_GROUP_W), 0) % SSD_HPG
    spread_x = jnp.where(lax.broadcasted_iota(i32, hrow.shape, 1) // SSD_HEAD_DIM == hrow, 1.0, 0.0).astype(bf16)
    prow = lax.broadcasted_iota(i32, (n_terms * SSD_HPG, SSD_HPG * q), 0) % SSD_HPG
    spread_q = jnp.where(lax.broadcasted_iota(i32, prow.shape, 1) // q == prow, 1.0, 0.0).astype(bf16)
    low_half = lax.broadcasted_iota(i32, (q, V7X_LANES), 1) < SSD_HEAD_DIM

    def spread(cols, mat):
        return _dot(jnp.concatenate(_split3(cols), axis=1), mat)

    def group(g):
        dt = jax.nn.softplus(dtq_ref[0, g] + dtb_ref[g])
        dtt = jax.nn.softplus(dtt_ref[0, g] + dtbt_ref[g])
        if valid_len is not None:
            dt = jnp.where(lax.broadcasted_iota(i32, dt.shape, 0) < valid_len, dt, 0.0)
            dtt = jnp.where(lax.broadcasted_iota(i32, dtt.shape, 1) < valid_len, dtt, 0.0)
        adt = dt * (-jnp.exp(alog_ref[g]))
        adtt = dtt * (-jnp.exp(alogt_ref[g]))
        acum = sum(_dot(tril, p) for p in _split3(adt))
        acumt = sum(_dot(p, triu) for p in _split3(adtt))
        last = acum[q - 1:q, :]
        col_q = spread(acum, spread_q)
        e_x = spread(jnp.exp(acum), spread_x)
        w_x = spread(jnp.exp(last - acum) * dt, spread_x)

        xoff = g * SSD_GROUP_W
        boff = SSD_INNER + g * SSD_STATE
        coff = SSD_INNER + SSD_GROUPS * SSD_STATE + g * SSD_STATE
        xg = xc_ref[:, pl.ds(xoff, SSD_GROUP_W)]
        bg = xc_ref[:, pl.ds(boff, SSD_STATE)].astype(bf16)
        cg = xc_ref[:, pl.ds(coff, SSD_STATE)].astype(bf16)
        cbm = _dot_nt(cg, bg)
        htg = ht_ref[:, pl.ds(xoff, SSD_GROUP_W)]
        yoff = _dot(cg, htg.astype(bf16))

        yds = []
        for pr in range(SSD_HPG // 2):
            xp = xg[:, pr * V7X_LANES:(pr + 1) * V7X_LANES]
            halves = (jnp.where(low_half, xp, 0.0).astype(bf16), jnp.where(low_half, 0.0, xp).astype(bf16))
            acc = None
            for r, xh in zip((2 * pr, 2 * pr + 1), halves):
                seg = col_q[:, r * q:(r + 1) * q] - acumt[r:r + 1, :]
                dec = jnp.exp(jnp.where(causal, seg, -jnp.inf))
                part = _dot((cbm * dec * dtt[r:r + 1, :]).astype(bf16), xh)
                acc = part if acc is None else acc + part
            yds.append(acc)
        yd = jnp.concatenate(yds, axis=1)
        yg = yd + yoff * e_x + dsk_ref[:, pl.ds(xoff, SSD_GROUP_W)] * xg

        ht_ref[:, pl.ds(xoff, SSD_GROUP_W)] = htg * e_x[q - 1:q, :] + _dot_tn(bg, (xg * w_x).astype(bf16))

        gz = yg * jax.nn.silu(z_ref[0, :, pl.ds(xoff, SSD_GROUP_W)])
        gzn = gz * lax.rsqrt(jnp.mean(gz * gz, axis=-1, keepdims=True) + NORM_EPS)
        yg_ref[0, :, pl.ds(xoff, SSD_GROUP_W)] = (gzn * gn_ref[:, pl.ds(xoff, SSD_GROUP_W)]).astype(yg_ref.dtype)

    for g in range(SSD_GROUPS):
        group(g)

    @pl.when(c == n_c - 1)
    def _():
        for g in range(SSD_GROUPS):
            sl = slice(g * SSD_GROUP_W, (g + 1) * SSD_GROUP_W)
            ssm_ref[0, sl, :] = ht_ref[:, sl].T


def _ssd_scan(proj, dt_raw, conv0, ssm0, conv_w, conv_b, dt_bias, a_log, d_skip, gate_norm, *, valid_len=None):
    b_, l_, _ = proj.shape
    q = SSD_CHUNK
    assert l_ % q == 0
    n_c = l_ // q
    g_, r_ = SSD_GROUPS, SSD_HPG
    dtq = dt_raw.reshape(b_, l_, g_, r_).transpose(0, 2, 1, 3)
    dtt = dt_raw.reshape(b_, l_, g_, r_).transpose(0, 2, 3, 1)
    h0t = ssm0.reshape(b_, SSD_INNER, SSD_STATE).transpose(0, 2, 1)
    c0 = jnp.pad(conv0, ((0, 0), (V7X_SUBLANES - (SSD_CONV - 1), 0), (0, 0)))
    dtb = dt_bias.reshape(g_, 1, r_)
    dtbt = dt_bias.reshape(g_, r_, 1)
    alog = a_log.reshape(g_, 1, r_)
    alogt = a_log.reshape(g_, r_, 1)
    dsk = jnp.repeat(d_skip, SSD_HEAD_DIM).reshape(1, SSD_INNER)
    full = lambda shape: pl.BlockSpec(shape, lambda b, c: (0,) * len(shape))
    in_specs = [
        pl.BlockSpec((1, q, SSD_INNER), lambda b, c: (b, c, 0)),
        pl.BlockSpec((1, q, SSD_INNER), lambda b, c: (b, c, 1)),
        pl.BlockSpec((1, q, SSD_BC_W), lambda b, c: (b, c, 2 * SSD_INNER // SSD_BC_W)),
        pl.BlockSpec((1, g_, q, r_), lambda b, c: (b, 0, c, 0)),
        pl.BlockSpec((1, g_, r_, q), lambda b, c: (b, 0, 0, c)),
        full((SSD_CONV, SSD_CONV_DIM)), full((1, SSD_CONV_DIM)),
        full((g_, 1, r_)), full((g_, r_, 1)), full((g_, 1, r_)), full((g_, r_, 1)), full((1, SSD_INNER)),
        full((1, SSD_INNER)),
        pl.BlockSpec((1, SSD_STATE, SSD_INNER), lambda b, c: (b, 0, 0)),
        pl.BlockSpec((1, V7X_SUBLANES, SSD_CONV_DIM), lambda b, c: (b, 0, 0)),
    ]
    out_specs = [
        pl.BlockSpec((1, q, SSD_INNER), lambda b, c: (b, c, 0)),
        pl.BlockSpec((1, SSD_INNER, SSD_STATE), lambda b, c: (b, 0, 0)),
    ]
    yg, ssm = pl.pallas_call(
        functools.partial(_ssd_body, valid_len=valid_len),
        grid=(b_, n_c),
        in_specs=in_specs,
        out_specs=out_specs,
        out_shape=[jax.ShapeDtypeStruct((b_, l_, SSD_INNER), bf16),
                   jax.ShapeDtypeStruct((b_, SSD_INNER, SSD_STATE), f32)],
        scratch_shapes=[pltpu.VMEM((SSD_STATE, SSD_INNER), f32),
                        pltpu.VMEM((q + 2 * V7X_SUBLANES, SSD_CONV_DIM), f32),
                        pltpu.VMEM((q, SSD_CONV_DIM), f32)],
        compiler_params=_cparams(2, 48 * MIB),
        name="ssd_scan",
    )(proj, proj, proj, dtq, dtt, conv_w, conv_b.reshape(1, -1), dtb, dtbt, alog, alogt, dsk,
      gate_norm.reshape(1, -1), h0t, c0)
    return yg, ssm.reshape(b_, SSD_HEADS, SSD_HEAD_DIM, SSD_STATE)


def _trig_body(pos_ref, invf_ref, sgn_ref, cos_ref, sin_ref):
    ang = pos_ref[...].astype(f32) * invf_ref[...]
    rot = lax.broadcasted_iota(i32, ang.shape, 1) < ROPE_DIM
    cos_ref[...] = jnp.where(rot, jnp.cos(ang), 1.0)
    sin_ref[...] = jnp.sin(ang) * sgn_ref[...]


def _trig_tables(pos):
    n = pos.shape[0]
    inv_freq = ROPE_THETA ** (-jnp.arange(ROPE_HALF, dtype=f32) * 2.0 / ROPE_DIM)
    pad = jnp.zeros((MOBA_HEAD_DIM - ROPE_DIM,), f32)
    invf = jnp.concatenate([inv_freq, inv_freq, pad]).reshape(1, MOBA_HEAD_DIM)
    sgn = jnp.concatenate([-jnp.ones((ROPE_HALF,), f32), jnp.ones((ROPE_HALF,), f32), pad]).reshape(1, MOBA_HEAD_DIM)
    out = jax.ShapeDtypeStruct((n, MOBA_HEAD_DIM), f32)
    return pl.pallas_call(_trig_body, out_shape=[out, out], name="rope_tables")(pos.reshape(n, 1).astype(i32), invf, sgn)


def _rotate(x, cos_t, sin_t):
    lane = lax.broadcasted_iota(i32, x.shape, 1)
    partner = jnp.where(lane < ROPE_HALF, pltpu.roll(x, MOBA_HEAD_DIM - ROPE_HALF, 1), pltpu.roll(x, ROPE_HALF, 1))
    return x * cos_t + partner * sin_t


def _prep_body(qkv_ref, cos_ref, sin_ref, q_ref, k_ref, kb_ref, vb_ref, *maybe_kmean, with_kmean):
    cos_t = cos_ref[...]
    sin_t = sin_ref[...]
    for h in range(MOBA_HEADS):
        sl = slice(h * MOBA_HEAD_DIM, (h + 1) * MOBA_HEAD_DIM)
        q_ref[:, sl] = _rotate(qkv_ref[:, sl], cos_t, sin_t)
        kr = _rotate(qkv_ref[:, ATTN_W + h * MOBA_HEAD_DIM:ATTN_W + (h + 1) * MOBA_HEAD_DIM], cos_t, sin_t)
        k_ref[:, sl] = kr
        kb_ref[:, sl] = kr.astype(bf16)
        if with_kmean:
            maybe_kmean[0][0, :, sl] = jnp.sum(kr, axis=0, keepdims=True) * (1.0 / MOBA_BLOCK)
    vb_ref[...] = qkv_ref[:, 2 * ATTN_W:3 * ATTN_W].astype(bf16)


def _moba_prep(qkv, cos_t, sin_t, *, rows, with_kmean):
    m = qkv.shape[0]
    n_tab = cos_t.shape[0] // rows
    row_spec = lambda w: pl.BlockSpec((rows, w), lambda i: (i, 0))
    tab_spec = pl.BlockSpec((rows, MOBA_HEAD_DIM), lambda i: (i % n_tab, 0))
    out_shape = [jax.ShapeDtypeStruct((m, ATTN_W), f32), jax.ShapeDtypeStruct((m, ATTN_W), f32),
                 jax.ShapeDtypeStruct((m, ATTN_W), bf16), jax.ShapeDtypeStruct((m, ATTN_W), bf16)]
    out_specs = [row_spec(ATTN_W)] * 4
    if with_kmean:
        out_shape.append(jax.ShapeDtypeStruct((m // rows, 1, ATTN_W), f32))
        out_specs.append(pl.BlockSpec((1, 1, ATTN_W), lambda i: (i, 0, 0)))
    return pl.pallas_call(
        functools.partial(_prep_body, with_kmean=with_kmean),
        grid=(m // rows,),
        in_specs=[row_spec(3 * ATTN_W), tab_spec, tab_spec],
        out_specs=out_specs,
        out_shape=out_shape,
        compiler_params=_cparams(1, 40 * MIB),
        name="moba_prep",
    )(qkv, cos_t, sin_t)


def _gate_scores(q, km):
    qh, qm, _ = _split3(q)
    kh, kmid, _ = _split3(km)
    return _dot_nt(qh, kh) + _dot_nt(qh, kmid) + _dot_nt(qm, kh)


def _top_blocks(gate, n_take):
    lane = lax.broadcasted_iota(i32, gate.shape, 1)
    picks = []
    for _ in range(n_take):
        mx = jnp.max(gate, axis=-1, keepdims=True)
        idx = jnp.min(jnp.where(gate == mx, lane, gate.shape[1]), axis=-1, keepdims=True)
        picks.append(idx)
        gate = jnp.where(lane == idx, -jnp.inf, gate)
    return picks


ATTN_HEADS_PER_STEP = 4
ATTN_MASKED = -1e30


def _attn_prompt_body(q_ref, kb_ref, vb_ref, km_ref, o_ref, s_ref, qa_ref):
    blk = MOBA_BLOCK
    hd = MOBA_HEAD_DIM
    cur = pl.program_id(2)
    n_blk = km_ref.shape[1]
    n_all = n_blk * blk
    scale = hd ** -0.5
    blk_id = lax.broadcasted_iota(i32, (n_blk, blk), 0)
    past = blk_id < cur
    causal = lax.broadcasted_iota(i32, (blk, blk), 1) <= lax.broadcasted_iota(i32, (blk, blk), 0)
    ones = jnp.ones((blk, hd), bf16)
    lane = lax.broadcasted_iota(i32, (blk, hd), 1)
    eye = jnp.where(lax.broadcasted_iota(i32, (n_blk, hd), 1) == lax.broadcasted_iota(i32, (n_blk, hd), 0),
                    1.0, 0.0).astype(bf16)
    own = pl.multiple_of(cur * blk, blk)
    own_slot = slice(n_all, n_all + blk)

    for hh in range(ATTN_HEADS_PER_STEP):
        hs = slice(hh * hd, (hh + 1) * hd)
        q = q_ref[:, hs]
        gate = jnp.where(past, _gate_scores(km_ref[0, :, hs], q), -jnp.inf)
        sel = jnp.zeros((n_blk, blk), i32)
        for _ in range(min(MOBA_TOPK, n_blk)):
            mx = jnp.max(gate, axis=0, keepdims=True)
            idx = jnp.min(jnp.where(gate == mx, blk_id, n_blk), axis=0, keepdims=True)
            sel = jnp.where((blk_id == idx) & past, 1, sel)
            gate = jnp.where(blk_id == idx, -jnp.inf, gate)
        bias_t = jnp.where(sel > 0, 0.0, ATTN_MASKED).astype(bf16)
        qa_ref[hh, :, 0:hd] = (q * scale).astype(bf16)
        qa_ref[hh, :, hd:2 * hd] = _dot_tn(bias_t, eye).astype(bf16)

    def attend(n_past):
        for hh in range(ATTN_HEADS_PER_STEP):
            hs = slice(hh * hd, (hh + 1) * hd)
            q_aug = qa_ref[hh]
            s = jnp.where(causal, _dot_nt(q_aug[:, 0:hd], kb_ref[pl.ds(own, blk), hs]), -jnp.inf)
            s_ref[hh, :, own_slot] = s
            m2 = jnp.maximum(s[:, :hd], s[:, hd:])
            for n in range(n_past):
                keys = slice(n * blk, (n + 1) * blk)
                k_aug = jnp.concatenate([kb_ref[keys, hs], jnp.where(lane == n, 1.0, 0.0).astype(bf16)], axis=1)
                s = _dot_nt(q_aug, k_aug)
                s_ref[hh, :, keys] = s
                m2 = jnp.maximum(m2, jnp.maximum(s[:, :hd], s[:, hd:]))
            m = jnp.broadcast_to(jnp.max(m2, axis=-1, keepdims=True), (blk, blk))
            p = jnp.exp(s_ref[hh, :, own_slot] - m).astype(bf16)
            acc = _dot(p, jnp.concatenate([vb_ref[pl.ds(own, blk), hs], ones], axis=1))
            for n in range(n_past):
                keys = slice(n * blk, (n + 1) * blk)
                p = jnp.exp(s_ref[hh, :, keys] - m).astype(bf16)
                acc = acc + _dot(p, jnp.concatenate([vb_ref[keys, hs], ones], axis=1))
            o_ref[:, hs] = (acc[:, :hd] / acc[:, hd:]).astype(o_ref.dtype)

    for pair in range((n_blk + 1) // 2):
        pl.when(cur // 2 == pair)(functools.partial(attend, min(2 * pair + 1, n_blk - 1)))


def _moba_attn_prompt(q, kb, vb, kmean, b_, l_):
    n_blk = l_ // MOBA_BLOCK
    w = ATTN_HEADS_PER_STEP * MOBA_HEAD_DIM
    return pl.pallas_call(
        _attn_prompt_body,
        grid=(b_, MOBA_HEADS // ATTN_HEADS_PER_STEP, n_blk),
        in_specs=[
            pl.BlockSpec((MOBA_BLOCK, w), lambda b, h, t: (b * n_blk + t, h)),
            pl.BlockSpec((l_, w), lambda b, h, t: (b, h)),
            pl.BlockSpec((l_, w), lambda b, h, t: (b, h)),
            pl.BlockSpec((1, n_blk, w), lambda b, h, t: (b, 0, h)),
        ],
        out_specs=pl.BlockSpec((MOBA_BLOCK, w), lambda b, h, t: (b * n_blk + t, h)),
        out_shape=jax.ShapeDtypeStruct((b_ * l_, ATTN_W), bf16),
        scratch_shapes=[pltpu.VMEM((ATTN_HEADS_PER_STEP, MOBA_BLOCK, l_ + MOBA_BLOCK), f32),
                        pltpu.VMEM((ATTN_HEADS_PER_STEP, MOBA_BLOCK, 2 * MOBA_HEAD_DIM), bf16)],
        compiler_params=_cparams(3, 32 * MIB),
        name="moba_attn_prompt",
    )(q, kb, vb, kmean)


PAGES_PER_BLOCK = MOBA_BLOCK // PAGE_SIZE
KMEAN_BLOCKS_PER_STEP = 8
KMEAN_PAGES_PER_STEP = KMEAN_BLOCKS_PER_STEP * PAGES_PER_BLOCK


def _kmean_body(pt_ref, *refs):
    page_refs, o_ref = refs[:-1], refs[-1]
    for blk in range(KMEAN_BLOCKS_PER_STEP):
        tot = None
        for p in range(PAGES_PER_BLOCK):
            part = jnp.sum(page_refs[blk * PAGES_PER_BLOCK + p][...], axis=0)
            tot = part if tot is None else tot + part
        tot = tot * (1.0 / MOBA_BLOCK)
        for h in range(MOBA_HEADS):
            o_ref[0, blk:blk + 1, h * MOBA_HEAD_DIM:(h + 1) * MOBA_HEAD_DIM] = tot[h:h + 1, :]


def _cache_kmean(cache, layer, page_table):
    b_, n_pages = page_table.shape
    n_steps = n_pages // KMEAN_PAGES_PER_STEP
    pt_flat = page_table.reshape(-1)

    def page_spec(p):
        return pl.BlockSpec((None, None, PAGE_SIZE, MOBA_HEADS, MOBA_HEAD_DIM),
                            lambda b, s, pt: (layer, pt[b * n_pages + s * KMEAN_PAGES_PER_STEP + p], 0, 0, 0))

    return pl.pallas_call(
        _kmean_body,
        grid_spec=pltpu.PrefetchScalarGridSpec(
            num_scalar_prefetch=1,
            grid=(b_, n_steps),
            in_specs=[page_spec(p) for p in range(KMEAN_PAGES_PER_STEP)],
            out_specs=pl.BlockSpec((1, KMEAN_BLOCKS_PER_STEP, ATTN_W), lambda b, s, pt: (b, s, 0)),
        ),
        out_shape=jax.ShapeDtypeStruct((b_, n_pages // PAGES_PER_BLOCK, ATTN_W), f32),
        compiler_params=_cparams(2, 40 * MIB),
        name="cache_kmean",
    )(pt_flat, *([cache] * KMEAN_PAGES_PER_STEP))


def _sample_topk_body(q_ref, km_ref, o_ref, g_ref):
    rows = q_ref.shape[1]
    for h in range(MOBA_HEADS):
        sl = slice(h * MOBA_HEAD_DIM, (h + 1) * MOBA_HEAD_DIM)
        g_ref[h * rows:(h + 1) * rows, :] = _gate_scores(q_ref[0, :, sl], km_ref[0, :, sl])
    picks = _top_blocks(g_ref[...], MOBA_TOPK)
    lane = lax.broadcasted_iota(i32, o_ref.shape[1:], 1)
    out = jnp.zeros(o_ref.shape[1:], i32)
    for s, idx in enumerate(picks):
        out = jnp.where(lane == s, idx, out)
    o_ref[0] = out


def _sample_topk(q8, kmean):
    b_, rows, _ = q8.shape
    n_blk = kmean.shape[1]
    return pl.pallas_call(
        _sample_topk_body,
        grid=(b_,),
        in_specs=[pl.BlockSpec((1, rows, ATTN_W), lambda b: (b, 0, 0)),
                  pl.BlockSpec((1, n_blk, ATTN_W), lambda b: (b, 0, 0))],
        out_specs=pl.BlockSpec((1, MOBA_HEADS * rows, V7X_LANES), lambda b: (b, 0, 0)),
        out_shape=jax.ShapeDtypeStruct((b_, MOBA_HEADS * rows, V7X_LANES), i32),
        scratch_shapes=[pltpu.VMEM((MOBA_HEADS * rows, n_blk), f32)],
        compiler_params=_cparams(1, 16 * MIB),
        name="sample_topk",
    )(q8, kmean)


def _attn_sample_body(sel_ref, pt_ref, q_ref, kn_ref, vn_ref, ck_hbm, cv_hbm, o_ref, kbuf, vbuf, sems,
                      *, layer, n_tok, n_pages):
    pages_per_tok = MOBA_TOPK * PAGES_PER_BLOCK
    keys_per_tok = pages_per_tok * PAGE_SIZE
    n_heads = pl.num_programs(1)
    step = pl.program_id(0) * n_heads + pl.program_id(1)
    n_steps = pl.num_programs(0) * n_heads
    slot = lax.rem(step, 2)

    def page_copies(step_, slot_, known_source):
        b_ = step_ // n_heads
        h_ = lax.rem(step_, n_heads) if known_source else 0
        out = []
        for t in range(n_tok):
            for sp in range(pages_per_tok):
                page = 0
                if known_source:
                    blk = sel_ref[step_ * (n_tok * MOBA_TOPK) + t * MOBA_TOPK + sp // PAGES_PER_BLOCK]
                    page = pt_ref[b_ * n_pages + blk * PAGES_PER_BLOCK + sp % PAGES_PER_BLOCK]
                dst = (slot_, t, pl.ds(sp * PAGE_SIZE, PAGE_SIZE))
                out.append(pltpu.make_async_copy(ck_hbm.at[layer, page, :, h_, :], kbuf.at[dst], sems.at[0, slot_]))
                out.append(pltpu.make_async_copy(cv_hbm.at[layer, page, :, h_, :], vbuf.at[dst], sems.at[1, slot_]))
        return out

    def start_all(copies):
        for n, cp in enumerate(copies):
            cp.start(priority=n % 2)

    @pl.when(step == 0)
    def _():
        start_all(page_copies(step, slot, True))

    @pl.when(step + 1 < n_steps)
    def _():
        start_all(page_copies(step + 1, 1 - slot, True))

    for cp in page_copies(step, slot, False):
        cp.wait()

    rows = q_ref.shape[1]
    scale = MOBA_HEAD_DIM ** -0.5
    qb = q_ref[0].astype(bf16)
    row = lax.broadcasted_iota(i32, (rows, keys_per_tok), 0)
    k_refs = [kbuf.at[slot, t] for t in range(n_tok)]
    v_refs = [vbuf.at[slot, t] for t in range(n_tok)]
    scores = []
    for t in range(n_tok):
        s = _dot_nt(qb, k_refs[t][...].astype(bf16)) * scale
        scores.append(jnp.where(row == t, s, -jnp.inf))
    r8 = lax.broadcasted_iota(i32, (rows, rows), 0)
    c8 = lax.broadcasted_iota(i32, (rows, rows), 1)
    s_own = _dot_nt(qb, kn_ref[0]) * scale
    s_own = jnp.where((c8 <= r8) & (c8 < n_tok), s_own, -jnp.inf)
    m = jnp.max(s_own, axis=-1, keepdims=True)
    for s in scores:
        m = jnp.maximum(m, jnp.max(s, axis=-1, keepdims=True))
    m = jnp.where(m == -jnp.inf, 0.0, m)
    p_own = jnp.exp(s_own - m)
    l = jnp.sum(p_own, axis=-1, keepdims=True)
    acc = _dot(p_own.astype(bf16), vn_ref[0])
    for t in range(n_tok):
        p = jnp.exp(scores[t] - m)
        l = l + jnp.sum(p, axis=-1, keepdims=True)
        acc = acc + _dot(p.astype(bf16), v_refs[t][...].astype(bf16))
    l = jnp.where(l == 0.0, 1.0, l)
    o_ref[0] = (acc / l).astype(o_ref.dtype)


def _moba_attn_sample(sel, page_table, q8, kn8, vn8, cache_k, cache_v, layer, n_tok):
    b_, rows, _ = q8.shape
    n_pages = page_table.shape[1]
    sel_flat = sel.reshape(-1).astype(i32)
    pt_flat = page_table.reshape(-1).astype(i32)
    head_spec = pl.BlockSpec((1, rows, MOBA_HEAD_DIM), lambda b, h, sel_r, pt_r: (b, 0, h))
    page_buf = pltpu.VMEM((2, n_tok, MOBA_TOPK * MOBA_BLOCK, MOBA_HEAD_DIM), f32)
    return pl.pallas_call(
        functools.partial(_attn_sample_body, layer=layer, n_tok=n_tok, n_pages=n_pages),
        grid_spec=pltpu.PrefetchScalarGridSpec(
            num_scalar_prefetch=2,
            grid=(b_, MOBA_HEADS),
            in_specs=[head_spec, head_spec, head_spec,
                      pl.BlockSpec(memory_space=pl.ANY), pl.BlockSpec(memory_space=pl.ANY)],
            out_specs=head_spec,
            scratch_shapes=[page_buf, page_buf, pltpu.SemaphoreType.DMA((2, 2))],
        ),
        out_shape=jax.ShapeDtypeStruct((b_, rows, ATTN_W), bf16),
        compiler_params=_cparams(2, 24 * MIB),
        name="moba_attn_sample",
    )(sel_flat, pt_flat, q8, kn8, vn8, cache_k, cache_v)


def _gmlp_body(uv_ref, lg_ref, lb_ref, ws_ref, bst_ref, o_ref, *maybe_v, with_v):
    n = GMLP_CHUNK
    vf = uv_ref[:, GMLP_DIM:2 * GMLP_DIM]
    mu = jnp.mean(vf, axis=-1, keepdims=True)
    var = jnp.mean(jnp.square(vf - mu), axis=-1, keepdims=True)
    v = (vf - mu) * lax.rsqrt(var + NORM_EPS) * lg_ref[...] + lb_ref[...]
    if with_v:
        maybe_v[0][...] = v
    ii = lax.broadcasted_iota(i32, (n, n), 0)
    jj = lax.broadcasted_iota(i32, (n, n), 1)
    for g in range(GMLP_GROUPS):
        sl = slice(g * GMLP_GROUP_DIM, (g + 1) * GMLP_GROUP_DIM)
        w = jnp.where(jj <= ii, ws_ref[g], 0.0).astype(bf16)
        s = _dot(w, v[:, sl].astype(bf16)) + bst_ref[:, g:g + 1]
        o_ref[:, sl] = (uv_ref[:, sl] * s).astype(o_ref.dtype)


def _gmlp_gate(uv, ln_g, ln_b, w_s, b_s, *, with_v):
    m = uv.shape[0]
    n = GMLP_CHUNK
    out_shape = [jax.ShapeDtypeStruct((m, GMLP_DIM), bf16)]
    out_specs = [pl.BlockSpec((n, GMLP_DIM), lambda i: (i, 0))]
    if with_v:
        out_shape.append(jax.ShapeDtypeStruct((m, GMLP_DIM), f32))
        out_specs.append(pl.BlockSpec((n, GMLP_DIM), lambda i: (i, 0)))
    return pl.pallas_call(
        functools.partial(_gmlp_body, with_v=with_v),
        grid=(m // n,),
        in_specs=[pl.BlockSpec((n, 2 * GMLP_DIM), lambda i: (i, 0)),
                  pl.BlockSpec((1, GMLP_DIM), lambda i: (0, 0)),
                  pl.BlockSpec((1, GMLP_DIM), lambda i: (0, 0)),
                  pl.BlockSpec((GMLP_GROUPS, n, n), lambda i: (0, 0, 0)),
                  pl.BlockSpec((n, GMLP_GROUPS), lambda i: (0, 0))],
        out_specs=out_specs,
        out_shape=out_shape,
        compiler_params=_cparams(1, 24 * MIB),
        name="gmlp_gate",
    )(uv, ln_g.reshape(1, -1), ln_b.reshape(1, -1), w_s, b_s.T)


def _pad_rows(x, rows):
    return jnp.pad(x, ((0, 0), (0, rows - x.shape[1]), (0, 0)))


def _ssd_group(proj, dt_raw, b_, l_, conv0, ssm0, scan_w):
    proj3 = proj.reshape(b_, l_, SSD_MAIN_W)
    dt3 = dt_raw.reshape(b_, l_, SSD_HEADS)
    if l_ < SSD_CONV - 1:
        new_conv = jnp.concatenate([conv0, proj3[:, :, SSD_INNER:]], axis=1)[:, -(SSD_CONV - 1):]
    else:
        new_conv = proj3[:, l_ - (SSD_CONV - 1):, SSD_INNER:]
    lp = -(-l_ // SSD_CHUNK) * SSD_CHUNK
    valid_len = None
    if lp != l_:
        assert lp == SSD_CHUNK
        proj3 = _pad_rows(proj3, lp)
        dt3 = _pad_rows(dt3, lp)
        valid_len = l_
    yg, ssm = _ssd_scan(proj3, dt3, conv0, ssm0, *scan_w, valid_len=valid_len)
    if lp != l_:
        yg = yg[:, :l_]
    return yg.reshape(b_ * l_, SSD_INNER), new_conv, ssm


def _ssd_layer(hp, hs, bp, lp_, bs, ls, gain, conv0_p, ssm0_p, conv0_s, ssm0_s, j, w_in_t, scan_w, w_out):
    proj_p, proj_s, dt_p, dt_s = _mm(hp, [w_in_t], layer=j, w_t=True, n_out=SSD_MAIN_W, tail=SSD_HEADS, gain=gain,
                                     side_x=hs)
    yg_p, conv_p, ssm_p = _ssd_group(proj_p, dt_p, bp, lp_, conv0_p, ssm0_p, scan_w)
    yg_s, conv_s, ssm_s = _ssd_group(proj_s, dt_s, bs, ls, conv0_s, ssm0_s, scan_w)
    hp, hs = _mm(yg_p, [w_out], layer=j, res=hp, side_x=yg_s, side_res=hs)
    return hp, hs, (conv_p, ssm_p), (conv_s, ssm_s)


def _ffn_layer(hp, hs, i, gain, w_gate, w_up, w_down):
    tp, ts = _mm(hp, [w_gate, w_up], layer=i, gain=gain, epi="swiglu", out_dtype=bf16, side_x=hs, max_rows=1024)
    return _mm(tp, [w_down], layer=i, res=hp, side_x=ts, side_res=hs)


def kernel(x_prompt, x_sample, state_ssm, state_conv, cache_k, cache_v, page_table, norm_mix, norm_ffn, norm_final, ssd_w_in, ssd_conv_w, ssd_conv_b, ssd_dt_bias, ssd_a_log, ssd_d, ssd_gate_norm, ssd_w_out, moba_w_qkv, moba_w_o, gmlp_w_in, gmlp_ln_g, gmlp_ln_b, gmlp_w_s, gmlp_b_s, gmlp_w_out, ffn_w_gate, ffn_w_up, ffn_w_down):
    bp, lp_, d = x_prompt.shape
    bs, ls, _ = x_sample.shape
    hp = x_prompt.reshape(bp * lp_, d)
    hs = x_sample.reshape(bs * ls, d)
    n_pages = page_table.shape[1]
    past = n_pages * PAGE_SIZE
    tok_rows = V7X_SUBLANES
    assert ls <= tok_rows and past % MOBA_BLOCK == 0 and lp_ % MOBA_BLOCK == 0

    ssm_p, ssm_s, conv_p, conv_s = [], [], [], []
    k_p, v_p, k_s, v_s, gv_s = [], [], [], [], []
    for i in range(DEPTH):
        kind = i % N_MIXERS
        j = i // N_MIXERS
        if kind == 0:
            w_in_t = jnp.swapaxes(ssd_w_in, 1, 2)
            scan_w = (ssd_conv_w[j], ssd_conv_b[j], ssd_dt_bias[j], ssd_a_log[j], ssd_d[j], ssd_gate_norm[j])
            zero_conv = jnp.zeros((bp, SSD_CONV - 1, SSD_CONV_DIM), f32)
            zero_ssm = jnp.zeros((bp, SSD_HEADS, SSD_HEAD_DIM, SSD_STATE), f32)
            hp, hs, (cp_, sp_), (cs_, ss_) = _ssd_layer(hp, hs, bp, lp_, bs, ls, norm_mix[i], zero_conv, zero_ssm,
                                                        state_conv[j], state_ssm[j], j, w_in_t, scan_w, ssd_w_out)
            conv_p.append(cp_)
            ssm_p.append(sp_)
            conv_s.append(cs_)
            ssm_s.append(ss_)
        elif kind == 1:
            qkv, qkv_s = _mm(hp, [moba_w_qkv], layer=j, gain=norm_mix[i], side_x=hs)
            cos_t, sin_t = _trig_tables(jnp.arange(lp_, dtype=i32))
            q, k, kb, vb, kmean = _moba_prep(qkv, cos_t, sin_t, rows=MOBA_BLOCK, with_kmean=True)
            o = _moba_attn_prompt(q, kb, vb, kmean.reshape(bp, lp_ // MOBA_BLOCK, ATTN_W), bp, lp_)
            k_p.append(k.reshape(bp, lp_, MOBA_HEADS, MOBA_HEAD_DIM))
            v_p.append(qkv[:, 2 * ATTN_W:].reshape(bp, lp_, MOBA_HEADS, MOBA_HEAD_DIM))
            pos_s = past + (jnp.arange(bs * ls, dtype=i32) % ls)
            cos_s, sin_s = _trig_tables(pos_s)
            q_s, kn, knb, vnb = _moba_prep(qkv_s, cos_s, sin_s, rows=bs * ls, with_kmean=False)
            kmean_c = _cache_kmean(cache_k, j, page_table)
            q8 = _pad_rows(q_s.reshape(bs, ls, ATTN_W), tok_rows)
            kn8 = _pad_rows(knb.reshape(bs, ls, ATTN_W), tok_rows)
            vn8 = _pad_rows(vnb.reshape(bs, ls, ATTN_W), tok_rows)
            top = _sample_topk(q8, kmean_c)
            sel = top.reshape(bs, MOBA_HEADS, tok_rows, V7X_LANES)[:, :, :ls, :MOBA_TOPK]
            o_s = _moba_attn_sample(sel, page_table, q8, kn8, vn8, cache_k, cache_v, j, ls)
            k_s.append(kn.reshape(bs, ls, MOBA_HEADS, MOBA_HEAD_DIM))
            v_s.append(qkv_s[:, 2 * ATTN_W:].reshape(bs, ls, MOBA_HEADS, MOBA_HEAD_DIM))
            hp, hs = _mm(o, [moba_w_o], layer=j, res=hp, side_x=o_s[:, :ls].reshape(bs * ls, ATTN_W), side_res=hs)
        else:
            w = (gmlp_ln_g[j], gmlp_ln_b[j], gmlp_w_s[j], gmlp_b_s[j])
            uv, uv_s = _mm(hp, [gmlp_w_in], layer=j, gain=norm_mix[i], epi="gelu", side_x=hs)
            (gated,) = _gmlp_gate(uv, *w, with_v=False)
            uv_s = _pad_rows(uv_s.reshape(bs, ls, 2 * GMLP_DIM), GMLP_CHUNK).reshape(bs * GMLP_CHUNK, 2 * GMLP_DIM)
            gated_s, v_rows = _gmlp_gate(uv_s, *w, with_v=True)
            gated_s = gated_s.reshape(bs, GMLP_CHUNK, GMLP_DIM)[:, :ls].reshape(bs * ls, GMLP_DIM)
            gv_s.append(v_rows.reshape(bs, GMLP_CHUNK, GMLP_DIM)[:, :ls])
            hp, hs = _mm(gated, [gmlp_w_out], layer=j, res=hp, side_x=gated_s, side_res=hs)
        hp, hs = _ffn_layer(hp, hs, i, norm_ffn[i], ffn_w_gate, ffn_w_up, ffn_w_down)
    y_prompt = _rms_norm(hp, norm_final).reshape(bp, lp_, d)
    y_sample = _rms_norm(hs, norm_final).reshape(bs, ls, d)
    return (y_prompt, y_sample, jnp.stack(ssm_p), jnp.stack(ssm_s), jnp.stack(conv_p), jnp.stack(conv_s),
            jnp.stack(k_p), jnp.stack(v_p), jnp.stack(k_s), jnp.stack(v_s), jnp.stack(gv_s))
```

```python
import functools
import math

import jax
import jax.numpy as jnp
from jax import lax
from jax.experimental import pallas as pl
from jax.experimental.pallas import tpu as pltpu

f32 = jnp.float32
bf16 = jnp.bfloat16
i32 = jnp.int32

D_MODEL = 2048
DEPTH = 4
PAGE_SIZE = 128
N_MIXERS = 3
NORM_EPS = 1e-6
SSD_INNER = 4096
SSD_HEAD_DIM = 64
SSD_HEADS = 64
SSD_GROUPS = 8
SSD_HPG = 8
SSD_STATE = 128
SSD_CONV = 4
SSD_CHUNK = 128
SSD_SHORT_CHUNK_ROWS = 16
SSD_GROUP_W = SSD_HPG * SSD_HEAD_DIM
SSD_BC_W = 2 * SSD_GROUPS * SSD_STATE
SSD_CONV_DIM = SSD_INNER + SSD_BC_W
SSD_MAIN_W = SSD_INNER + SSD_CONV_DIM
MOBA_HEADS = 16
MOBA_HEAD_DIM = 128
MOBA_BLOCK = 256
MOBA_TOPK = 3
ROPE_THETA = 500000.0
ROPE_DIM = 32
ROPE_HALF = ROPE_DIM // 2
ATTN_W = MOBA_HEADS * MOBA_HEAD_DIM
GMLP_DIM = 2048
GMLP_GROUPS = 16
GMLP_GROUP_DIM = 128
GMLP_CHUNK = 128

V7X_LANES = 128
V7X_SUBLANES = 8
V7X_VMEM_LIMIT_BYTES = 60 * 1024 * 1024
MIB = 1024 * 1024


def _cparams(n_grid, vmem_bytes):
    return pltpu.CompilerParams(
        dimension_semantics=("arbitrary",) * n_grid,
        vmem_limit_bytes=int(min(max(vmem_bytes, 16 * MIB), V7X_VMEM_LIMIT_BYTES)),
    )


def _split3(x):
    hi = x.astype(bf16)
    r1 = x - hi.astype(f32)
    mid = r1.astype(bf16)
    lo = (r1 - mid.astype(f32)).astype(bf16)
    return hi, mid, lo


def _dot(a, b):
    return jnp.dot(a, b, preferred_element_type=f32)


def _dot_nt(a, b):
    return lax.dot_general(a, b, (((1,), (1,)), ((), ())), preferred_element_type=f32)


def _dot_tn(a, b):
    return lax.dot_general(a, b, (((0,), (0,)), ((), ())), preferred_element_type=f32)


def _gelu_exact(x):
    return 0.5 * x * (1.0 + lax.erf(x * (1.0 / math.sqrt(2.0))))


MM_MAX_ROWS = 2048
MM_VMEM_BUDGET = 50 * MIB


def _mm_body(*refs, n_w, has_gain, has_res, epi, tm, ms, w_t, staged, tail):
    it = iter(refs)
    x_ref = next(it)
    sx_ref = next(it) if ms else None
    g_ref = next(it) if has_gain else None
    w_refs = [next(it) for _ in range(n_w)]
    wt_ref = next(it) if tail else None
    r_ref = next(it) if has_res else None
    sr_ref = next(it) if (has_res and ms) else None
    o_ref = next(it)
    so_ref = next(it) if ms else None
    ot_ref = next(it) if tail else None
    sot_ref = next(it) if (tail and ms) else None
    xs_ref = next(it) if staged else None
    i = pl.program_id(0)
    j = pl.program_id(1)
    mxu = _dot_nt if w_t else _dot

    if staged:
        def prep(v):
            v = v.astype(f32)
            if has_gain:
                v = v * lax.rsqrt(jnp.mean(v * v, axis=-1, keepdims=True) + NORM_EPS) * g_ref[...]
            return v.astype(bf16)

        @pl.when(j == 0)
        def _():
            xs_ref[0:tm, :] = prep(x_ref[...])

        if ms:
            @pl.when((j == 0) & (i == 0))
            def _():
                xs_ref[tm:tm + ms, :] = prep(sx_ref[...])

    wb = [w[...].astype(bf16) for w in w_refs]

    def product(xb):
        accs = [mxu(xb, w) for w in wb]
        if epi == "swiglu":
            return jax.nn.silu(accs[0]) * accs[1]
        if epi == "gelu":
            return _gelu_exact(accs[0])
        return accs[0]

    def store(main, side):
        if has_res:
            main = r_ref[...] + main
        o_ref[...] = main.astype(o_ref.dtype)
        if side is not None:
            if has_res:
                side = sr_ref[...] + side
            so_ref[...] = side.astype(so_ref.dtype)

    def run(with_side):
        if staged:
            rows = tm + ms if with_side else tm
            out = product(xs_ref[0:rows, :])
            store(out[0:tm], out[tm:rows] if with_side else None)
            if tail:
                @pl.when(j == 0)
                def _():
                    t = _dot_nt(xs_ref[0:rows, :], wt_ref[...].astype(bf16))
                    ot_ref[...] = t[0:tm]
                    if with_side:
                        sot_ref[...] = t[tm:rows]
        else:
            store(product(x_ref[...]), product(sx_ref[...]) if with_side else None)

    if ms:
        pl.when(i == 0)(functools.partial(run, True))

        @pl.when(i > 0)
        def _():
            run(False)
            so_ref[...] = jnp.zeros(so_ref.shape, so_ref.dtype)
            if tail:
                @pl.when(j == 0)
                def _():
                    sot_ref[...] = jnp.zeros(sot_ref.shape, sot_ref.dtype)
    else:
        run(False)


def _mm(x, ws, *, layer=None, w_t=False, n_out=None, tail=0, gain=None, res=None, side_x=None, side_res=None,
        epi="none", out_dtype=f32, max_rows=MM_MAX_ROWS):
    m, k = x.shape
    n_total = ws[0].shape[-2] if w_t else ws[0].shape[-1]
    n = n_out if n_out is not None else n_total
    ms = 0 if side_x is None else side_x.shape[0]
    n_w = len(ws)
    has_gain = gain is not None
    has_res = res is not None
    staged = has_gain or x.dtype != bf16
    assert not tail or (w_t and staged and n % tail == 0)
    out_b = jnp.dtype(out_dtype).itemsize

    def vmem_estimate(tm_, tn_, x_bufs_):
        return (x_bufs_ * tm_ * k * x.dtype.itemsize + (staged * (tm_ + ms) * k * 2) + n_w * k * tn_ * (2 * 4 + 2)
                + tm_ * tn_ * (2 * out_b + (8 if has_res else 0)) + n_w * (tm_ + ms) * tn_ * 4 + 2 * MIB)

    row_blocks = [t for t in (2048, 1024, 512, 256, 128, 64, 32, 16, 8) if t <= max_rows and m % t == 0] or [m]
    col_blocks = [t for t in (512, 256) if n % t == 0] or [n]
    tm, tn, x_bufs = next(((a, b, c) for a in row_blocks for b in col_blocks for c in ((2, 1) if staged else (2,))
                           if vmem_estimate(a, b, c) <= MM_VMEM_BUDGET), (row_blocks[-1], col_blocks[-1], 1))
    assert m % tm == 0 and n % tn == 0
    once = dict(pipeline_mode=pl.Buffered(1)) if x_bufs == 1 else {}
    in_specs = [pl.BlockSpec((tm, k), lambda i, j: (i, 0), **once)]
    args = [x]
    if ms:
        in_specs.append(pl.BlockSpec((ms, k), lambda i, j: (0, 0), **once))
        args.append(side_x)
    if has_gain:
        in_specs.append(pl.BlockSpec((1, k), lambda i, j: (0, 0), **once))
        args.append(gain.reshape(1, k).astype(f32))
    for w in ws:
        if w_t:
            in_specs.append(pl.BlockSpec((None, tn, k), lambda i, j: (layer, j, 0)))
        elif layer is None:
            in_specs.append(pl.BlockSpec((k, tn), lambda i, j: (0, j)))
        else:
            in_specs.append(pl.BlockSpec((None, k, tn), lambda i, j: (layer, 0, j)))
        args.append(w)
    if tail:
        in_specs.append(pl.BlockSpec((None, tail, k), lambda i, j: (layer, n // tail, 0)))
        args.append(ws[0])
    if has_res:
        in_specs.append(pl.BlockSpec((tm, tn), lambda i, j: (i, j)))
        args.append(res)
        if ms:
            in_specs.append(pl.BlockSpec((ms, tn), lambda i, j: (0, j)))
            args.append(side_res)
    out_specs = [pl.BlockSpec((tm, tn), lambda i, j: (i, j))]
    out_shape = [jax.ShapeDtypeStruct((m, n), out_dtype)]
    if ms:
        out_specs.append(pl.BlockSpec((ms, tn), lambda i, j: (i, j)))
        out_shape.append(jax.ShapeDtypeStruct((m // tm * ms, n), out_dtype))
    if tail:
        out_specs.append(pl.BlockSpec((tm, tail), lambda i, j: (i, 0)))
        out_shape.append(jax.ShapeDtypeStruct((m, tail), f32))
        if ms:
            out_specs.append(pl.BlockSpec((ms, tail), lambda i, j: (i, 0)))
            out_shape.append(jax.ShapeDtypeStruct((m // tm * ms, tail), f32))
    body = functools.partial(_mm_body, n_w=n_w, has_gain=has_gain, has_res=has_res, epi=epi, tm=tm, ms=ms, w_t=w_t,
                             staged=staged, tail=tail)
    outs = pl.pallas_call(
        body,
        grid=(m // tm, n // tn),
        in_specs=in_specs,
        out_specs=out_specs,
        out_shape=out_shape,
        scratch_shapes=[pltpu.VMEM((tm + ms, k), bf16)] if staged else [],
        compiler_params=_cparams(2, vmem_estimate(tm, tn, x_bufs) + 6 * MIB),
        name="mm_" + epi,
    )(*args)
    outs = list(outs)
    if ms:
        outs[1] = outs[1][:ms]
        if tail:
            outs[3] = outs[3][:ms]
    return outs if len(outs) > 1 else outs[0]


def _rms_body(x_ref, g_ref, o_ref):
    x = x_ref[...]
    o_ref[...] = x * lax.rsqrt(jnp.mean(x * x, axis=-1, keepdims=True) + NORM_EPS) * g_ref[...]


def _rms_norm(x, gain):
    m, d = x.shape
    tm = min(m, 512)
    return pl.pallas_call(
        _rms_body,
        grid=(m // tm,),
        in_specs=[pl.BlockSpec((tm, d), lambda i: (i, 0)), pl.BlockSpec((1, d), lambda i: (0, 0))],
        out_specs=pl.BlockSpec((tm, d), lambda i: (i, 0)),
        out_shape=jax.ShapeDtypeStruct((m, d), f32),
        compiler_params=_cparams(1, 4 * tm * d * 4 + 4 * MIB),
        name="final_rms",
    )(x, gain.reshape(1, d))


def _ssd_body(z_ref, xp_ref, bc_ref, dtq_ref, dtt_ref, cw_ref, cb_ref, dtb_ref, dtbt_ref, alog_ref, alogt_ref,
              dsk_ref, gn_ref, h0_ref, c0_ref, yg_ref, ssm_ref, ht_ref, xs_ref, xc_ref, *, valid_len):
    q = z_ref.shape[1]
    c = pl.program_id(1)
    n_c = pl.num_programs(1)

    @pl.when(c == 0)
    def _():
        ht_ref[...] = h0_ref[0]
        xs_ref[0:V7X_SUBLANES, :] = c0_ref[0]

    xs_ref[V7X_SUBLANES:V7X_SUBLANES + q, 0:SSD_INNER] = xp_ref[0]
    xs_ref[V7X_SUBLANES:V7X_SUBLANES + q, SSD_INNER:SSD_CONV_DIM] = bc_ref[0]
    lane_step = 512
    for s in range(SSD_CONV_DIM // lane_step):
        sl = slice(s * lane_step, (s + 1) * lane_step)
        acc = cb_ref[:, sl] + cw_ref[SSD_CONV - 1:SSD_CONV, sl] * xs_ref[V7X_SUBLANES:V7X_SUBLANES + q, sl]
        for kk in range(1, SSD_CONV):
            acc = acc + cw_ref[SSD_CONV - 1 - kk:SSD_CONV - kk, sl] * xs_ref[V7X_SUBLANES - kk:V7X_SUBLANES - kk + q, sl]
        xc_ref[:, sl] = jax.nn.silu(acc)
    xs_ref[0:V7X_SUBLANES, :] = xs_ref[q:q + V7X_SUBLANES, :]

    ii = lax.broadcasted_iota(i32, (q, q), 0)
    jj = lax.broadcasted_iota(i32, (q, q), 1)
    causal = ii >= jj
    tril = jnp.where(causal, 1.0, 0.0).astype(bf16)
    triu = jnp.where(jj >= ii, 1.0, 0.0).astype(bf16)

    n_terms = 3
    hrow = lax.broadcasted_iota(i32, (n_terms * SSD_HPG, SSD_GROUP_W), 0) % SSD_HPG
    spread_x = jnp.where(lax.broadcasted_iota(i32, hrow.shape, 1) // SSD_HEAD_DIM == hrow, 1.0, 0.0).astype(bf16)
    prow = lax.broadcasted_iota(i32, (n_terms * SSD_HPG, SSD_HPG * q), 0) % SSD_HPG
    spread_q = jnp.where(lax.broadcasted_iota(i32, prow.shape, 1) // q == prow, 1.0, 0.0).astype(bf16)
    low_half = lax.broadcasted_iota(i32, (q, V7X_LANES), 1) < SSD_HEAD_DIM

    def spread(cols, mat):
        return _dot(jnp.concatenate(_split3(cols), axis=1), mat)

    def group(g):
        dt = jax.nn.softplus(dtq_ref[0, g] + dtb_ref[g])
        dtt = jax.nn.softplus(dtt_ref[0, g] + dtbt_ref[g])
        if valid_len is not None:
            dt = jnp.where(lax.broadcasted_iota(i32, dt.shape, 0) < valid_len, dt, 0.0)
            dtt = jnp.where(lax.broadcasted_iota(i32, dtt.shape, 1) < valid_len, dtt, 0.0)
        adt = dt * (-jnp.exp(alog_ref[g]))
        adtt = dtt * (-jnp.exp(alogt_ref[g]))
        acum = sum(_dot(tril, p) for p in _split3(adt))
        acumt = sum(_dot(p, triu) for p in _split3(adtt))
        last = acum[q - 1:q, :]
        col_q = spread(acum, spread_q)
        e_x = spread(jnp.exp(acum), spread_x)
        w_x = spread(jnp.exp(last - acum) * dt, spread_x)

        xoff = g * SSD_GROUP_W
        boff = SSD_INNER + g * SSD_STATE
        coff = SSD_INNER + SSD_GROUPS * SSD_STATE + g * SSD_STATE
        xg = xc_ref[:, pl.ds(xoff, SSD_GROUP_W)]
        bg = xc_ref[:, pl.ds(boff, SSD_STATE)].astype(bf16)
        cg = xc_ref[:, pl.ds(coff, SSD_STATE)].astype(bf16)
        cbm = _dot_nt(cg, bg)
        htg = ht_ref[:, pl.ds(xoff, SSD_GROUP_W)]
        yoff = _dot(cg, htg.astype(bf16))

        yds = []
        for pr in range(SSD_HPG // 2):
            xp = xg[:, pr * V7X_LANES:(pr + 1) * V7X_LANES]
            halves = (jnp.where(low_half, xp, 0.0).astype(bf16), jnp.where(low_half, 0.0, xp).astype(bf16))
            acc = None
            for r, xh in zip((2 * pr, 2 * pr + 1), halves):
                seg = col_q[:, r * q:(r + 1) * q] - acumt[r:r + 1, :]
                dec = jnp.exp(jnp.where(causal, seg, -jnp.inf))
                part = _dot((cbm * dec * dtt[r:r + 1, :]).astype(bf16), xh)
                acc = part if acc is None else acc + part
            yds.append(acc)
        yd = jnp.concatenate(yds, axis=1)
        yg = yd + yoff * e_x + dsk_ref[:, pl.ds(xoff, SSD_GROUP_W)] * xg

        ht_ref[:, pl.ds(xoff, SSD_GROUP_W)] = htg * e_x[q - 1:q, :] + _dot_tn(bg, (xg * w_x).astype(bf16))

        gz = yg * jax.nn.silu(z_ref[0, :, pl.ds(xoff, SSD_GROUP_W)])
        gzn = gz * lax.rsqrt(jnp.mean(gz * gz, axis=-1, keepdims=True) + NORM_EPS)
        yg_ref[0, :, pl.ds(xoff, SSD_GROUP_W)] = (gzn * gn_ref[:, pl.ds(xoff, SSD_GROUP_W)]).astype(yg_ref.dtype)

    for g in range(SSD_GROUPS):
        group(g)

    @pl.when(c == n_c - 1)
    def _():
        for g in range(SSD_GROUPS):
            sl = slice(g * SSD_GROUP_W, (g + 1) * SSD_GROUP_W)
            ssm_ref[0, sl, :] = ht_ref[:, sl].T


def _ssd_scan(proj, dt_raw, conv0, ssm0, conv_w, conv_b, dt_bias, a_log, d_skip, gate_norm, *, valid_len=None,
              chunk=SSD_CHUNK):
    b_, l_, _ = proj.shape
    q = chunk
    assert l_ % q == 0
    n_c = l_ // q
    g_, r_ = SSD_GROUPS, SSD_HPG
    dtq = dt_raw.reshape(b_, l_, g_, r_).transpose(0, 2, 1, 3)
    dtt = dt_raw.reshape(b_, l_, g_, r_).transpose(0, 2, 3, 1)
    h0t = ssm0.reshape(b_, SSD_INNER, SSD_STATE).transpose(0, 2, 1)
    c0 = jnp.pad(conv0, ((0, 0), (V7X_SUBLANES - (SSD_CONV - 1), 0), (0, 0)))
    dtb = dt_bias.reshape(g_, 1, r_)
    dtbt = dt_bias.reshape(g_, r_, 1)
    alog = a_log.reshape(g_, 1, r_)
    alogt = a_log.reshape(g_, r_, 1)
    dsk = jnp.repeat(d_skip, SSD_HEAD_DIM).reshape(1, SSD_INNER)
    full = lambda shape: pl.BlockSpec(shape, lambda b, c: (0,) * len(shape))
    in_specs = [
        pl.BlockSpec((1, q, SSD_INNER), lambda b, c: (b, c, 0)),
        pl.BlockSpec((1, q, SSD_INNER), lambda b, c: (b, c, 1)),
        pl.BlockSpec((1, q, SSD_BC_W), lambda b, c: (b, c, 2 * SSD_INNER // SSD_BC_W)),
        pl.BlockSpec((1, g_, q, r_), lambda b, c: (b, 0, c, 0)),
        pl.BlockSpec((1, g_, r_, q), lambda b, c: (b, 0, 0, c)),
        full((SSD_CONV, SSD_CONV_DIM)), full((1, SSD_CONV_DIM)),
        full((g_, 1, r_)), full((g_, r_, 1)), full((g_, 1, r_)), full((g_, r_, 1)), full((1, SSD_INNER)),
        full((1, SSD_INNER)),
        pl.BlockSpec((1, SSD_STATE, SSD_INNER), lambda b, c: (b, 0, 0)),
        pl.BlockSpec((1, V7X_SUBLANES, SSD_CONV_DIM), lambda b, c: (b, 0, 0)),
    ]
    out_specs = [
        pl.BlockSpec((1, q, SSD_INNER), lambda b, c: (b, c, 0)),
        pl.BlockSpec((1, SSD_INNER, SSD_STATE), lambda b, c: (b, 0, 0)),
    ]
    yg, ssm = pl.pallas_call(
        functools.partial(_ssd_body, valid_len=valid_len),
        grid=(b_, n_c),
        in_specs=in_specs,
        out_specs=out_specs,
        out_shape=[jax.ShapeDtypeStruct((b_, l_, SSD_INNER), bf16),
                   jax.ShapeDtypeStruct((b_, SSD_INNER, SSD_STATE), f32)],
        scratch_shapes=[pltpu.VMEM((SSD_STATE, SSD_INNER), f32),
                        pltpu.VMEM((q + 2 * V7X_SUBLANES, SSD_CONV_DIM), f32),
                        pltpu.VMEM((q, SSD_CONV_DIM), f32)],
        compiler_params=_cparams(2, 48 * MIB),
        name="ssd_scan",
    )(proj, proj, proj, dtq, dtt, conv_w, conv_b.reshape(1, -1), dtb, dtbt, alog, alogt, dsk,
      gate_norm.reshape(1, -1), h0t, c0)
    return yg, ssm.reshape(b_, SSD_HEADS, SSD_HEAD_DIM, SSD_STATE)


def _trig_body(pos_ref, invf_ref, sgn_ref, cos_ref, sin_ref):
    ang = pos_ref[...].astype(f32) * invf_ref[...]
    rot = lax.broadcasted_iota(i32, ang.shape, 1) < ROPE_DIM
    cos_ref[...] = jnp.where(rot, jnp.cos(ang), 1.0)
    sin_ref[...] = jnp.sin(ang) * sgn_ref[...]


def _trig_tables(pos):
    n = pos.shape[0]
    inv_freq = ROPE_THETA ** (-jnp.arange(ROPE_HALF, dtype=f32) * 2.0 / ROPE_DIM)
    pad = jnp.zeros((MOBA_HEAD_DIM - ROPE_DIM,), f32)
    invf = jnp.concatenate([inv_freq, inv_freq, pad]).reshape(1, MOBA_HEAD_DIM)
    sgn = jnp.concatenate([-jnp.ones((ROPE_HALF,), f32), jnp.ones((ROPE_HALF,), f32), pad]).reshape(1, MOBA_HEAD_DIM)
    out = jax.ShapeDtypeStruct((n, MOBA_HEAD_DIM), f32)
    return pl.pallas_call(_trig_body, out_shape=[out, out], name="rope_tables")(pos.reshape(n, 1).astype(i32), invf, sgn)


def _rotate(x, cos_t, sin_t):
    lane = lax.broadcasted_iota(i32, x.shape, 1)
    partner = jnp.where(lane < ROPE_HALF, pltpu.roll(x, MOBA_HEAD_DIM - ROPE_HALF, 1), pltpu.roll(x, ROPE_HALF, 1))
    return x * cos_t + partner * sin_t


def _prep_body(qkv_ref, cos_ref, sin_ref, q_ref, k_ref, kb_ref, vb_ref, *maybe_kmean, with_kmean):
    cos_t = cos_ref[...]
    sin_t = sin_ref[...]
    for h in range(MOBA_HEADS):
        sl = slice(h * MOBA_HEAD_DIM, (h + 1) * MOBA_HEAD_DIM)
        q_ref[:, sl] = _rotate(qkv_ref[:, sl], cos_t, sin_t)
        kr = _rotate(qkv_ref[:, ATTN_W + h * MOBA_HEAD_DIM:ATTN_W + (h + 1) * MOBA_HEAD_DIM], cos_t, sin_t)
        k_ref[:, sl] = kr
        kb_ref[:, sl] = kr.astype(bf16)
        if with_kmean:
            maybe_kmean[0][0, :, sl] = jnp.sum(kr, axis=0, keepdims=True) * (1.0 / MOBA_BLOCK)
    vb_ref[...] = qkv_ref[:, 2 * ATTN_W:3 * ATTN_W].astype(bf16)


def _moba_prep(qkv, cos_t, sin_t, *, rows, with_kmean):
    m = qkv.shape[0]
    n_tab = cos_t.shape[0] // rows
    row_spec = lambda w: pl.BlockSpec((rows, w), lambda i: (i, 0))
    tab_spec = pl.BlockSpec((rows, MOBA_HEAD_DIM), lambda i: (i % n_tab, 0))
    out_shape = [jax.ShapeDtypeStruct((m, ATTN_W), f32), jax.ShapeDtypeStruct((m, ATTN_W), f32),
                 jax.ShapeDtypeStruct((m, ATTN_W), bf16), jax.ShapeDtypeStruct((m, ATTN_W), bf16)]
    out_specs = [row_spec(ATTN_W)] * 4
    if with_kmean:
        out_shape.append(jax.ShapeDtypeStruct((m // rows, 1, ATTN_W), f32))
        out_specs.append(pl.BlockSpec((1, 1, ATTN_W), lambda i: (i, 0, 0)))
    return pl.pallas_call(
        functools.partial(_prep_body, with_kmean=with_kmean),
        grid=(m // rows,),
        in_specs=[row_spec(3 * ATTN_W), tab_spec, tab_spec],
        out_specs=out_specs,
        out_shape=out_shape,
        compiler_params=_cparams(1, 40 * MIB),
        name="moba_prep",
    )(qkv, cos_t, sin_t)


def _gate_scores(q, km):
    qh, qm, _ = _split3(q)
    kh, kmid, _ = _split3(km)
    return _dot_nt(qh, kh) + _dot_nt(qh, kmid) + _dot_nt(qm, kh)


def _top_blocks(gate, n_take):
    lane = lax.broadcasted_iota(i32, gate.shape, 1)
    picks = []
    for _ in range(n_take):
        mx = jnp.max(gate, axis=-1, keepdims=True)
        idx = jnp.min(jnp.where(gate == mx, lane, gate.shape[1]), axis=-1, keepdims=True)
        picks.append(idx)
        gate = jnp.where(lane == idx, -jnp.inf, gate)
    return picks


ATTN_HEADS_PER_STEP = 4
ATTN_MASKED = -1e30


def _attn_prompt_body(q_ref, kb_ref, vb_ref, km_ref, o_ref, s_ref, qa_ref):
    blk = MOBA_BLOCK
    hd = MOBA_HEAD_DIM
    cur = pl.program_id(2)
    n_blk = km_ref.shape[1]
    n_all = n_blk * blk
    scale = hd ** -0.5
    blk_id = lax.broadcasted_iota(i32, (n_blk, blk), 0)
    past = blk_id < cur
    causal = lax.broadcasted_iota(i32, (blk, blk), 1) <= lax.broadcasted_iota(i32, (blk, blk), 0)
    ones = jnp.ones((blk, hd), bf16)
    lane = lax.broadcasted_iota(i32, (blk, hd), 1)
    eye = jnp.where(lax.broadcasted_iota(i32, (n_blk, hd), 1) == lax.broadcasted_iota(i32, (n_blk, hd), 0),
                    1.0, 0.0).astype(bf16)
    own = pl.multiple_of(cur * blk, blk)
    own_slot = slice(n_all, n_all + blk)

    for hh in range(ATTN_HEADS_PER_STEP):
        hs = slice(hh * hd, (hh + 1) * hd)
        q = q_ref[:, hs]
        gate = jnp.where(past, _gate_scores(km_ref[0, :, hs], q), -jnp.inf)
        sel = jnp.zeros((n_blk, blk), i32)
        for _ in range(min(MOBA_TOPK, n_blk)):
            mx = jnp.max(gate, axis=0, keepdims=True)
            idx = jnp.min(jnp.where(gate == mx, blk_id, n_blk), axis=0, keepdims=True)
            sel = jnp.where((blk_id == idx) & past, 1, sel)
            gate = jnp.where(blk_id == idx, -jnp.inf, gate)
        bias_t = jnp.where(sel > 0, 0.0, ATTN_MASKED).astype(bf16)
        qa_ref[hh, :, 0:hd] = (q * scale).astype(bf16)
        qa_ref[hh, :, hd:2 * hd] = _dot_tn(bias_t, eye).astype(bf16)

    def attend(n_past):
        for hh in range(ATTN_HEADS_PER_STEP):
            hs = slice(hh * hd, (hh + 1) * hd)
            q_aug = qa_ref[hh]
            s = jnp.where(causal, _dot_nt(q_aug[:, 0:hd], kb_ref[pl.ds(own, blk), hs]), -jnp.inf)
            s_ref[hh, :, own_slot] = s
            m2 = jnp.maximum(s[:, :hd], s[:, hd:])
            for n in range(n_past):
                keys = slice(n * blk, (n + 1) * blk)
                k_aug = jnp.concatenate([kb_ref[keys, hs], jnp.where(lane == n, 1.0, 0.0).astype(bf16)], axis=1)
                s = _dot_nt(q_aug, k_aug)
                s_ref[hh, :, keys] = s
                m2 = jnp.maximum(m2, jnp.maximum(s[:, :hd], s[:, hd:]))
            m = jnp.broadcast_to(jnp.max(m2, axis=-1, keepdims=True), (blk, blk))
            p = jnp.exp(s_ref[hh, :, own_slot] - m).astype(bf16)
            acc = _dot(p, jnp.concatenate([vb_ref[pl.ds(own, blk), hs], ones], axis=1))
            for n in range(n_past):
                keys = slice(n * blk, (n + 1) * blk)
                p = jnp.exp(s_ref[hh, :, keys] - m).astype(bf16)
                acc = acc + _dot(p, jnp.concatenate([vb_ref[keys, hs], ones], axis=1))
            o_ref[:, hs] = (acc[:, :hd] / acc[:, hd:]).astype(o_ref.dtype)

    for pair in range((n_blk + 1) // 2):
        pl.when(cur // 2 == pair)(functools.partial(attend, min(2 * pair + 1, n_blk - 1)))


def _moba_attn_prompt(q, kb, vb, kmean, b_, l_):
    n_blk = l_ // MOBA_BLOCK
    w = ATTN_HEADS_PER_STEP * MOBA_HEAD_DIM
    return pl.pallas_call(
        _attn_prompt_body,
        grid=(b_, MOBA_HEADS // ATTN_HEADS_PER_STEP, n_blk),
        in_specs=[
            pl.BlockSpec((MOBA_BLOCK, w), lambda b, h, t: (b * n_blk + t, h)),
            pl.BlockSpec((l_, w), lambda b, h, t: (b, h)),
            pl.BlockSpec((l_, w), lambda b, h, t: (b, h)),
            pl.BlockSpec((1, n_blk, w), lambda b, h, t: (b, 0, h)),
        ],
        out_specs=pl.BlockSpec((MOBA_BLOCK, w), lambda b, h, t: (b * n_blk + t, h)),
        out_shape=jax.ShapeDtypeStruct((b_ * l_, ATTN_W), bf16),
        scratch_shapes=[pltpu.VMEM((ATTN_HEADS_PER_STEP, MOBA_BLOCK, l_ + MOBA_BLOCK), f32),
                        pltpu.VMEM((ATTN_HEADS_PER_STEP, MOBA_BLOCK, 2 * MOBA_HEAD_DIM), bf16)],
        compiler_params=_cparams(3, 32 * MIB),
        name="moba_attn_prompt",
    )(q, kb, vb, kmean)


PAGES_PER_BLOCK = MOBA_BLOCK // PAGE_SIZE
KMEAN_BLOCKS_PER_STEP = 8
KMEAN_PAGES_PER_STEP = KMEAN_BLOCKS_PER_STEP * PAGES_PER_BLOCK


def _kmean_body(pt_ref, *refs):
    page_refs, o_ref = refs[:-1], refs[-1]
    for blk in range(KMEAN_BLOCKS_PER_STEP):
        tot = None
        for p in range(PAGES_PER_BLOCK):
            part = jnp.sum(page_refs[blk * PAGES_PER_BLOCK + p][...], axis=0)
            tot = part if tot is None else tot + part
        tot = tot * (1.0 / MOBA_BLOCK)
        for h in range(MOBA_HEADS):
            o_ref[0, blk:blk + 1, h * MOBA_HEAD_DIM:(h + 1) * MOBA_HEAD_DIM] = tot[h:h + 1, :]


def _cache_kmean(cache, layer, page_table):
    b_, n_pages = page_table.shape
    n_steps = n_pages // KMEAN_PAGES_PER_STEP
    pt_flat = page_table.reshape(-1)

    def page_spec(p):
        return pl.BlockSpec((None, None, PAGE_SIZE, MOBA_HEADS, MOBA_HEAD_DIM),
                            lambda b, s, pt: (layer, pt[b * n_pages + s * KMEAN_PAGES_PER_STEP + p], 0, 0, 0))

    return pl.pallas_call(
        _kmean_body,
        grid_spec=pltpu.PrefetchScalarGridSpec(
            num_scalar_prefetch=1,
            grid=(b_, n_steps),
            in_specs=[page_spec(p) for p in range(KMEAN_PAGES_PER_STEP)],
            out_specs=pl.BlockSpec((1, KMEAN_BLOCKS_PER_STEP, ATTN_W), lambda b, s, pt: (b, s, 0)),
        ),
        out_shape=jax.ShapeDtypeStruct((b_, n_pages // PAGES_PER_BLOCK, ATTN_W), f32),
        compiler_params=_cparams(2, 40 * MIB),
        name="cache_kmean",
    )(pt_flat, *([cache] * KMEAN_PAGES_PER_STEP))


def _sample_topk_body(q_ref, km_ref, o_ref, g_ref):
    rows = q_ref.shape[1]
    for h in range(MOBA_HEADS):
        sl = slice(h * MOBA_HEAD_DIM, (h + 1) * MOBA_HEAD_DIM)
        g_ref[h * rows:(h + 1) * rows, :] = _gate_scores(q_ref[0, :, sl], km_ref[0, :, sl])
    picks = _top_blocks(g_ref[...], MOBA_TOPK)
    lane = lax.broadcasted_iota(i32, o_ref.shape[1:], 1)
    out = jnp.zeros(o_ref.shape[1:], i32)
    for s, idx in enumerate(picks):
        out = jnp.where(lane == s, idx, out)
    o_ref[0] = out


def _sample_topk(q8, kmean):
    b_, rows, _ = q8.shape
    n_blk = kmean.shape[1]
    return pl.pallas_call(
        _sample_topk_body,
        grid=(b_,),
        in_specs=[pl.BlockSpec((1, rows, ATTN_W), lambda b: (b, 0, 0)),
                  pl.BlockSpec((1, n_blk, ATTN_W), lambda b: (b, 0, 0))],
        out_specs=pl.BlockSpec((1, MOBA_HEADS * rows, V7X_LANES), lambda b: (b, 0, 0)),
        out_shape=jax.ShapeDtypeStruct((b_, MOBA_HEADS * rows, V7X_LANES), i32),
        scratch_shapes=[pltpu.VMEM((MOBA_HEADS * rows, n_blk), f32)],
        compiler_params=_cparams(1, 16 * MIB),
        name="sample_topk",
    )(q8, kmean)


def _attn_sample_body(sel_ref, pt_ref, q_ref, kn_ref, vn_ref, ck_hbm, cv_hbm, o_ref, kbuf, vbuf, sems,
                      *, layer, n_tok, n_pages):
    pages_per_tok = MOBA_TOPK * PAGES_PER_BLOCK
    keys_per_tok = pages_per_tok * PAGE_SIZE
    n_heads = pl.num_programs(1)
    step = pl.program_id(0) * n_heads + pl.program_id(1)
    n_steps = pl.num_programs(0) * n_heads
    slot = lax.rem(step, 2)

    def page_copies(step_, slot_, known_source):
        b_ = step_ // n_heads
        h_ = lax.rem(step_, n_heads) if known_source else 0
        out = []
        for t in range(n_tok):
            for sp in range(pages_per_tok):
                page = 0
                if known_source:
                    blk = sel_ref[step_ * (n_tok * MOBA_TOPK) + t * MOBA_TOPK + sp // PAGES_PER_BLOCK]
                    page = pt_ref[b_ * n_pages + blk * PAGES_PER_BLOCK + sp % PAGES_PER_BLOCK]
                dst = (slot_, t, pl.ds(sp * PAGE_SIZE, PAGE_SIZE))
                out.append(pltpu.make_async_copy(ck_hbm.at[layer, page, :, h_, :], kbuf.at[dst], sems.at[0, slot_]))
                out.append(pltpu.make_async_copy(cv_hbm.at[layer, page, :, h_, :], vbuf.at[dst], sems.at[1, slot_]))
        return out

    def start_all(copies):
        for n, cp in enumerate(copies):
            cp.start(priority=n % 2)

    @pl.when(step == 0)
    def _():
        start_all(page_copies(step, slot, True))

    @pl.when(step + 1 < n_steps)
    def _():
        start_all(page_copies(step + 1, 1 - slot, True))

    for cp in page_copies(step, slot, False):
        cp.wait()

    rows = q_ref.shape[1]
    scale = MOBA_HEAD_DIM ** -0.5
    qb = q_ref[0].astype(bf16)
    row = lax.broadcasted_iota(i32, (rows, keys_per_tok), 0)
    k_refs = [kbuf.at[slot, t] for t in range(n_tok)]
    v_refs = [vbuf.at[slot, t] for t in range(n_tok)]
    scores = []
    for t in range(n_tok):
        s = _dot_nt(qb, k_refs[t][...].astype(bf16)) * scale
        scores.append(jnp.where(row == t, s, -jnp.inf))
    r8 = lax.broadcasted_iota(i32, (rows, rows), 0)
    c8 = lax.broadcasted_iota(i32, (rows, rows), 1)
    s_own = _dot_nt(qb, kn_ref[0]) * scale
    s_own = jnp.where((c8 <= r8) & (c8 < n_tok), s_own, -jnp.inf)
    m = jnp.max(s_own, axis=-1, keepdims=True)
    for s in scores:
        m = jnp.maximum(m, jnp.max(s, axis=-1, keepdims=True))
    m = jnp.where(m == -jnp.inf, 0.0, m)
    p_own = jnp.exp(s_own - m)
    l = jnp.sum(p_own, axis=-1, keepdims=True)
    acc = _dot(p_own.astype(bf16), vn_ref[0])
    for t in range(n_tok):
        p = jnp.exp(scores[t] - m)
        l = l + jnp.sum(p, axis=-1, keepdims=True)
        acc = acc + _dot(p.astype(bf16), v_refs[t][...].astype(bf16))
    l = jnp.where(l == 0.0, 1.0, l)
    o_ref[0] = (acc / l).astype(o_ref.dtype)


def _moba_attn_sample(sel, page_table, q8, kn8, vn8, cache_k, cache_v, layer, n_tok):
    b_, rows, _ = q8.shape
    n_pages = page_table.shape[1]
    sel_flat = sel.reshape(-1).astype(i32)
    pt_flat = page_table.reshape(-1).astype(i32)
    head_spec = pl.BlockSpec((1, rows, MOBA_HEAD_DIM), lambda b, h, sel_r, pt_r: (b, 0, h))
    page_buf = pltpu.VMEM((2, n_tok, MOBA_TOPK * MOBA_BLOCK, MOBA_HEAD_DIM), f32)
    return pl.pallas_call(
        functools.partial(_attn_sample_body, layer=layer, n_tok=n_tok, n_pages=n_pages),
        grid_spec=pltpu.PrefetchScalarGridSpec(
            num_scalar_prefetch=2,
            grid=(b_, MOBA_HEADS),
            in_specs=[head_spec, head_spec, head_spec,
                      pl.BlockSpec(memory_space=pl.ANY), pl.BlockSpec(memory_space=pl.ANY)],
            out_specs=head_spec,
            scratch_shapes=[page_buf, page_buf, pltpu.SemaphoreType.DMA((2, 2))],
        ),
        out_shape=jax.ShapeDtypeStruct((b_, rows, ATTN_W), bf16),
        compiler_params=_cparams(2, 24 * MIB),
        name="moba_attn_sample",
    )(sel_flat, pt_flat, q8, kn8, vn8, cache_k, cache_v)


def _gmlp_body(uv_ref, lg_ref, lb_ref, ws_ref, bst_ref, o_ref, *maybe_v, with_v):
    n = GMLP_CHUNK
    vf = uv_ref[:, GMLP_DIM:2 * GMLP_DIM]
    mu = jnp.mean(vf, axis=-1, keepdims=True)
    var = jnp.mean(jnp.square(vf - mu), axis=-1, keepdims=True)
    v = (vf - mu) * lax.rsqrt(var + NORM_EPS) * lg_ref[...] + lb_ref[...]
    if with_v:
        maybe_v[0][...] = v
    ii = lax.broadcasted_iota(i32, (n, n), 0)
    jj = lax.broadcasted_iota(i32, (n, n), 1)
    for g in range(GMLP_GROUPS):
        sl = slice(g * GMLP_GROUP_DIM, (g + 1) * GMLP_GROUP_DIM)
        w = jnp.where(jj <= ii, ws_ref[g], 0.0).astype(bf16)
        s = _dot(w, v[:, sl].astype(bf16)) + bst_ref[:, g:g + 1]
        o_ref[:, sl] = (uv_ref[:, sl] * s).astype(o_ref.dtype)


def _gmlp_gate(uv, ln_g, ln_b, w_s, b_s, *, with_v):
    m = uv.shape[0]
    n = GMLP_CHUNK
    out_shape = [jax.ShapeDtypeStruct((m, GMLP_DIM), bf16)]
    out_specs = [pl.BlockSpec((n, GMLP_DIM), lambda i: (i, 0))]
    if with_v:
        out_shape.append(jax.ShapeDtypeStruct((m, GMLP_DIM), f32))
        out_specs.append(pl.BlockSpec((n, GMLP_DIM), lambda i: (i, 0)))
    return pl.pallas_call(
        functools.partial(_gmlp_body, with_v=with_v),
        grid=(m // n,),
        in_specs=[pl.BlockSpec((n, 2 * GMLP_DIM), lambda i: (i, 0)),
                  pl.BlockSpec((1, GMLP_DIM), lambda i: (0, 0)),
                  pl.BlockSpec((1, GMLP_DIM), lambda i: (0, 0)),
                  pl.BlockSpec((GMLP_GROUPS, n, n), lambda i: (0, 0, 0)),
                  pl.BlockSpec((n, GMLP_GROUPS), lambda i: (0, 0))],
        out_specs=out_specs,
        out_shape=out_shape,
        compiler_params=_cparams(1, 24 * MIB),
        name="gmlp_gate",
    )(uv, ln_g.reshape(1, -1), ln_b.reshape(1, -1), w_s, b_s.T)


def _pad_rows(x, rows):
    return jnp.pad(x, ((0, 0), (0, rows - x.shape[1]), (0, 0)))


def _ssd_group(proj, dt_raw, b_, l_, conv0, ssm0, scan_w):
    proj3 = proj.reshape(b_, l_, SSD_MAIN_W)
    dt3 = dt_raw.reshape(b_, l_, SSD_HEADS)
    if l_ < SSD_CONV - 1:
        new_conv = jnp.concatenate([conv0, proj3[:, :, SSD_INNER:]], axis=1)[:, -(SSD_CONV - 1):]
    else:
        new_conv = proj3[:, l_ - (SSD_CONV - 1):, SSD_INNER:]
    chunk = SSD_CHUNK if l_ >= SSD_CHUNK else -(-l_ // SSD_SHORT_CHUNK_ROWS) * SSD_SHORT_CHUNK_ROWS
    lp = -(-l_ // chunk) * chunk
    valid_len = None
    if lp != l_:
        assert lp == chunk
        proj3 = _pad_rows(proj3, lp)
        dt3 = _pad_rows(dt3, lp)
        valid_len = l_
    yg, ssm = _ssd_scan(proj3, dt3, conv0, ssm0, *scan_w, valid_len=valid_len, chunk=chunk)
    if lp != l_:
        yg = yg[:, :l_]
    return yg.reshape(b_ * l_, SSD_INNER), new_conv, ssm


def _ssd_layer(hp, hs, bp, lp_, bs, ls, gain, conv0_p, ssm0_p, conv0_s, ssm0_s, j, w_in_t, scan_w, w_out):
    proj_p, proj_s, dt_p, dt_s = _mm(hp, [w_in_t], layer=j, w_t=True, n_out=SSD_MAIN_W, tail=SSD_HEADS, gain=gain,
                                     side_x=hs)
    yg_p, conv_p, ssm_p = _ssd_group(proj_p, dt_p, bp, lp_, conv0_p, ssm0_p, scan_w)
    yg_s, conv_s, ssm_s = _ssd_group(proj_s, dt_s, bs, ls, conv0_s, ssm0_s, scan_w)
    hp, hs = _mm(yg_p, [w_out], layer=j, res=hp, side_x=yg_s, side_res=hs)
    return hp, hs, (conv_p, ssm_p), (conv_s, ssm_s)


def _ffn_layer(hp, hs, i, gain, w_gate, w_up, w_down):
    tp, ts = _mm(hp, [w_gate, w_up], layer=i, gain=gain, epi="swiglu", out_dtype=bf16, side_x=hs, max_rows=1024)
    return _mm(tp, [w_down], layer=i, res=hp, side_x=ts, side_res=hs)


def kernel(x_prompt, x_sample, state_ssm, state_conv, cache_k, cache_v, page_table, norm_mix, norm_ffn, norm_final, ssd_w_in, ssd_conv_w, ssd_conv_b, ssd_dt_bias, ssd_a_log, ssd_d, ssd_gate_norm, ssd_w_out, moba_w_qkv, moba_w_o, gmlp_w_in, gmlp_ln_g, gmlp_ln_b, gmlp_w_s, gmlp_b_s, gmlp_w_out, ffn_w_gate, ffn_w_up, ffn_w_down):
    bp, lp_, d = x_prompt.shape
    bs, ls, _ = x_sample.shape
    hp = x_prompt.reshape(bp * lp_, d)
    hs = x_sample.reshape(bs * ls, d)
    n_pages = page_table.shape[1]
    past = n_pages * PAGE_SIZE
    tok_rows = V7X_SUBLANES
    assert ls <= tok_rows and past % MOBA_BLOCK == 0 and lp_ % MOBA_BLOCK == 0

    ssm_p, ssm_s, conv_p, conv_s = [], [], [], []
    k_p, v_p, k_s, v_s, gv_s = [], [], [], [], []
    for i in range(DEPTH):
        kind = i % N_MIXERS
        j = i // N_MIXERS
        if kind == 0:
            w_in_t = jnp.swapaxes(ssd_w_in, 1, 2)
            scan_w = (ssd_conv_w[j], ssd_conv_b[j], ssd_dt_bias[j], ssd_a_log[j], ssd_d[j], ssd_gate_norm[j])
            zero_conv = jnp.zeros((bp, SSD_CONV - 1, SSD_CONV_DIM), f32)
            zero_ssm = jnp.zeros((bp, SSD_HEADS, SSD_HEAD_DIM, SSD_STATE), f32)
            hp, hs, (cp_, sp_), (cs_, ss_) = _ssd_layer(hp, hs, bp, lp_, bs, ls, norm_mix[i], zero_conv, zero_ssm,
                                                        state_conv[j], state_ssm[j], j, w_in_t, scan_w, ssd_w_out)
            conv_p.append(cp_)
            ssm_p.append(sp_)
            conv_s.append(cs_)
            ssm_s.append(ss_)
        elif kind == 1:
            qkv, qkv_s = _mm(hp, [moba_w_qkv], layer=j, gain=norm_mix[i], side_x=hs)
            cos_t, sin_t = _trig_tables(jnp.arange(lp_, dtype=i32))
            q, k, kb, vb, kmean = _moba_prep(qkv, cos_t, sin_t, rows=MOBA_BLOCK, with_kmean=True)
            o = _moba_attn_prompt(q, kb, vb, kmean.reshape(bp, lp_ // MOBA_BLOCK, ATTN_W), bp, lp_)
            k_p.append(k.reshape(bp, lp_, MOBA_HEADS, MOBA_HEAD_DIM))
            v_p.append(qkv[:, 2 * ATTN_W:].reshape(bp, lp_, MOBA_HEADS, MOBA_HEAD_DIM))
            pos_s = past + (jnp.arange(bs * ls, dtype=i32) % ls)
            cos_s, sin_s = _trig_tables(pos_s)
            q_s, kn, knb, vnb = _moba_prep(qkv_s, cos_s, sin_s, rows=bs * ls, with_kmean=False)
            kmean_c = _cache_kmean(cache_k, j, page_table)
            q8 = _pad_rows(q_s.reshape(bs, ls, ATTN_W), tok_rows)
            kn8 = _pad_rows(knb.reshape(bs, ls, ATTN_W), tok_rows)
            vn8 = _pad_rows(vnb.reshape(bs, ls, ATTN_W), tok_rows)
            top = _sample_topk(q8, kmean_c)
            sel = top.reshape(bs, MOBA_HEADS, tok_rows, V7X_LANES)[:, :, :ls, :MOBA_TOPK]
            o_s = _moba_attn_sample(sel, page_table, q8, kn8, vn8, cache_k, cache_v, j, ls)
            k_s.append(kn.reshape(bs, ls, MOBA_HEADS, MOBA_HEAD_DIM))
            v_s.append(qkv_s[:, 2 * ATTN_W:].reshape(bs, ls, MOBA_HEADS, MOBA_HEAD_DIM))
            hp, hs = _mm(o, [moba_w_o], layer=j, res=hp, side_x=o_s[:, :ls].reshape(bs * ls, ATTN_W), side_res=hs)
        else:
            w = (gmlp_ln_g[j], gmlp_ln_b[j], gmlp_w_s[j], gmlp_b_s[j])
            uv, uv_s = _mm(hp, [gmlp_w_in], layer=j, gain=norm_mix[i], epi="gelu", side_x=hs)
            (gated,) = _gmlp_gate(uv, *w, with_v=False)
            uv_s = _pad_rows(uv_s.reshape(bs, ls, 2 * GMLP_DIM), GMLP_CHUNK).reshape(bs * GMLP_CHUNK, 2 * GMLP_DIM)
            gated_s, v_rows = _gmlp_gate(uv_s, *w, with_v=True)
            gated_s = gated_s.reshape(bs, GMLP_CHUNK, GMLP_DIM)[:, :ls].reshape(bs * ls, GMLP_DIM)
            gv_s.append(v_rows.reshape(bs, GMLP_CHUNK, GMLP_DIM)[:, :ls])
            hp, hs = _mm(gated, [gmlp_w_out], layer=j, res=hp, side_x=gated_s, side_res=hs)
        hp, hs = _ffn_layer(hp, hs, i, norm_ffn[i], ffn_w_gate, ffn_w_up, ffn_w_down)
    y_prompt = _rms_norm(hp, norm_final).reshape(bp, lp_, d)
    y_sample = _rms_norm(hs, norm_final).reshape(bs, ls, d)
    return (y_prompt, y_sample, jnp.stack(ssm_p), jnp.stack(ssm_s), jnp.stack(conv_p), jnp.stack(conv_s),
            jnp.stack(k_p), jnp.stack(v_p), jnp.stack(k_s), jnp.stack(v_s), jnp.stack(gv_s))
```

```python
import functools
import math

import jax
import jax.numpy as jnp
from jax import lax
from jax.experimental import pallas as pl
from jax.experimental.pallas import tpu as pltpu

f32 = jnp.float32
bf16 = jnp.bfloat16
i32 = jnp.int32

D_MODEL = 2048
DEPTH = 4
PAGE_SIZE = 128
N_MIXERS = 3
NORM_EPS = 1e-6
SSD_INNER = 4096
SSD_HEAD_DIM = 64
SSD_HEADS = 64
SSD_GROUPS = 8
SSD_HPG = 8
SSD_STATE = 128
SSD_CONV = 4
SSD_CHUNK = 128
SSD_SHORT_CHUNK_ROWS = 16
SSD_GROUP_W = SSD_HPG * SSD_HEAD_DIM
SSD_BC_W = 2 * SSD_GROUPS * SSD_STATE
SSD_CONV_DIM = SSD_INNER + SSD_BC_W
SSD_MAIN_W = SSD_INNER + SSD_CONV_DIM
MOBA_HEADS = 16
MOBA_HEAD_DIM = 128
MOBA_BLOCK = 256
MOBA_TOPK = 3
ROPE_THETA = 500000.0
ROPE_DIM = 32
ROPE_HALF = ROPE_DIM // 2
ATTN_W = MOBA_HEADS * MOBA_HEAD_DIM
GMLP_DIM = 2048
GMLP_GROUPS = 16
GMLP_GROUP_DIM = 128
GMLP_CHUNK = 128
GMLP_CHUNKS_PER_STEP = 4

V7X_LANES = 128
V7X_SUBLANES = 8
V7X_VMEM_LIMIT_BYTES = 60 * 1024 * 1024
MIB = 1024 * 1024


def _cparams(n_grid, vmem_bytes):
    return pltpu.CompilerParams(
        dimension_semantics=("arbitrary",) * n_grid,
        vmem_limit_bytes=int(min(max(vmem_bytes, 16 * MIB), V7X_VMEM_LIMIT_BYTES)),
    )


def _split3(x):
    hi = x.astype(bf16)
    r1 = x - hi.astype(f32)
    mid = r1.astype(bf16)
    lo = (r1 - mid.astype(f32)).astype(bf16)
    return hi, mid, lo


def _dot(a, b):
    return jnp.dot(a, b, preferred_element_type=f32)


def _dot_nt(a, b):
    return lax.dot_general(a, b, (((1,), (1,)), ((), ())), preferred_element_type=f32)


def _dot_tn(a, b):
    return lax.dot_general(a, b, (((0,), (0,)), ((), ())), preferred_element_type=f32)


def _gelu_exact(x):
    return 0.5 * x * (1.0 + lax.erf(x * (1.0 / math.sqrt(2.0))))


MM_MAX_ROWS = 2048
MM_VMEM_BUDGET = 50 * MIB


def _mm_body(*refs, n_w, has_gain, has_res, epi, tm, ms, w_t, staged, tail):
    it = iter(refs)
    x_ref = next(it)
    sx_ref = next(it) if ms else None
    g_ref = next(it) if has_gain else None
    w_refs = [next(it) for _ in range(n_w)]
    wt_ref = next(it) if tail else None
    r_ref = next(it) if has_res else None
    sr_ref = next(it) if (has_res and ms) else None
    o_ref = next(it)
    so_ref = next(it) if ms else None
    ot_ref = next(it) if tail else None
    sot_ref = next(it) if (tail and ms) else None
    xs_ref = next(it) if staged else None
    i = pl.program_id(0)
    j = pl.program_id(1)
    mxu = _dot_nt if w_t else _dot

    if staged:
        def prep(v):
            v = v.astype(f32)
            if has_gain:
                v = v * lax.rsqrt(jnp.mean(v * v, axis=-1, keepdims=True) + NORM_EPS) * g_ref[...]
            return v.astype(bf16)

        @pl.when(j == 0)
        def _():
            xs_ref[0:tm, :] = prep(x_ref[...])

        if ms:
            @pl.when((j == 0) & (i == 0))
            def _():
                xs_ref[tm:tm + ms, :] = prep(sx_ref[...])

    wb = [w[...].astype(bf16) for w in w_refs]

    def product(xb):
        accs = [mxu(xb, w) for w in wb]
        if epi == "swiglu":
            return jax.nn.silu(accs[0]) * accs[1]
        if epi == "gelu":
            return _gelu_exact(accs[0])
        return accs[0]

    def store(main, side):
        if has_res:
            main = r_ref[...] + main
        o_ref[...] = main.astype(o_ref.dtype)
        if side is not None:
            if has_res:
                side = sr_ref[...] + side
            so_ref[...] = side.astype(so_ref.dtype)

    def run(with_side):
        if staged:
            rows = tm + ms if with_side else tm
            out = product(xs_ref[0:rows, :])
            store(out[0:tm], out[tm:rows] if with_side else None)
            if tail:
                @pl.when(j == 0)
                def _():
                    t = _dot_nt(xs_ref[0:rows, :], wt_ref[...].astype(bf16))
                    ot_ref[...] = t[0:tm]
                    if with_side:
                        sot_ref[...] = t[tm:rows]
        else:
            store(product(x_ref[...]), product(sx_ref[...]) if with_side else None)

    if ms:
        pl.when(i == 0)(functools.partial(run, True))

        @pl.when(i > 0)
        def _():
            run(False)
            so_ref[...] = jnp.zeros(so_ref.shape, so_ref.dtype)
            if tail:
                @pl.when(j == 0)
                def _():
                    sot_ref[...] = jnp.zeros(sot_ref.shape, sot_ref.dtype)
    else:
        run(False)


def _mm(x, ws, *, layer=None, w_t=False, n_out=None, tail=0, gain=None, res=None, side_x=None, side_res=None,
        epi="none", out_dtype=f32, max_rows=MM_MAX_ROWS):
    m, k = x.shape
    n_total = ws[0].shape[-2] if w_t else ws[0].shape[-1]
    n = n_out if n_out is not None else n_total
    ms = 0 if side_x is None else side_x.shape[0]
    n_w = len(ws)
    has_gain = gain is not None
    has_res = res is not None
    staged = has_gain or x.dtype != bf16
    assert not tail or (w_t and staged and n % tail == 0)
    out_b = jnp.dtype(out_dtype).itemsize

    def vmem_estimate(tm_, tn_, x_bufs_):
        return (x_bufs_ * tm_ * k * x.dtype.itemsize + (staged * (tm_ + ms) * k * 2) + n_w * k * tn_ * (2 * 4 + 2)
                + tm_ * tn_ * (2 * out_b + (8 if has_res else 0)) + n_w * (tm_ + ms) * tn_ * 4 + 2 * MIB)

    row_blocks = [t for t in (2048, 1024, 512, 256, 128, 64, 32, 16, 8) if t <= max_rows and m % t == 0] or [m]
    col_blocks = [t for t in (512, 256) if n % t == 0] or [n]
    tm, tn, x_bufs = next(((a, b, c) for a in row_blocks for b in col_blocks for c in ((2, 1) if staged else (2,))
                           if vmem_estimate(a, b, c) <= MM_VMEM_BUDGET), (row_blocks[-1], col_blocks[-1], 1))
    assert m % tm == 0 and n % tn == 0
    once = dict(pipeline_mode=pl.Buffered(1)) if x_bufs == 1 else {}
    in_specs = [pl.BlockSpec((tm, k), lambda i, j: (i, 0), **once)]
    args = [x]
    if ms:
        in_specs.append(pl.BlockSpec((ms, k), lambda i, j: (0, 0), **once))
        args.append(side_x)
    if has_gain:
        in_specs.append(pl.BlockSpec((1, k), lambda i, j: (0, 0), **once))
        args.append(gain.reshape(1, k).astype(f32))
    for w in ws:
        if w_t:
            in_specs.append(pl.BlockSpec((None, tn, k), lambda i, j: (layer, j, 0)))
        elif layer is None:
            in_specs.append(pl.BlockSpec((k, tn), lambda i, j: (0, j)))
        else:
            in_specs.append(pl.BlockSpec((None, k, tn), lambda i, j: (layer, 0, j)))
        args.append(w)
    if tail:
        in_specs.append(pl.BlockSpec((None, tail, k), lambda i, j: (layer, n // tail, 0)))
        args.append(ws[0])
    if has_res:
        in_specs.append(pl.BlockSpec((tm, tn), lambda i, j: (i, j)))
        args.append(res)
        if ms:
            in_specs.append(pl.BlockSpec((ms, tn), lambda i, j: (0, j)))
            args.append(side_res)
    out_specs = [pl.BlockSpec((tm, tn), lambda i, j: (i, j))]
    out_shape = [jax.ShapeDtypeStruct((m, n), out_dtype)]
    if ms:
        out_specs.append(pl.BlockSpec((ms, tn), lambda i, j: (i, j)))
        out_shape.append(jax.ShapeDtypeStruct((m // tm * ms, n), out_dtype))
    if tail:
        out_specs.append(pl.BlockSpec((tm, tail), lambda i, j: (i, 0)))
        out_shape.append(jax.ShapeDtypeStruct((m, tail), f32))
        if ms:
            out_specs.append(pl.BlockSpec((ms, tail), lambda i, j: (i, 0)))
            out_shape.append(jax.ShapeDtypeStruct((m // tm * ms, tail), f32))
    body = functools.partial(_mm_body, n_w=n_w, has_gain=has_gain, has_res=has_res, epi=epi, tm=tm, ms=ms, w_t=w_t,
                             staged=staged, tail=tail)
    outs = pl.pallas_call(
        body,
        grid=(m // tm, n // tn),
        in_specs=in_specs,
        out_specs=out_specs,
        out_shape=out_shape,
        scratch_shapes=[pltpu.VMEM((tm + ms, k), bf16)] if staged else [],
        compiler_params=_cparams(2, vmem_estimate(tm, tn, x_bufs) + 6 * MIB),
        name="mm_" + epi,
    )(*args)
    outs = list(outs)
    if ms:
        outs[1] = outs[1][:ms]
        if tail:
            outs[3] = outs[3][:ms]
    return outs if len(outs) > 1 else outs[0]


def _rms_body(x_ref, g_ref, o_ref):
    x = x_ref[...]
    o_ref[...] = x * lax.rsqrt(jnp.mean(x * x, axis=-1, keepdims=True) + NORM_EPS) * g_ref[...]


def _rms_norm(x, gain):
    m, d = x.shape
    tm = min(m, 512)
    return pl.pallas_call(
        _rms_body,
        grid=(m // tm,),
        in_specs=[pl.BlockSpec((tm, d), lambda i: (i, 0)), pl.BlockSpec((1, d), lambda i: (0, 0))],
        out_specs=pl.BlockSpec((tm, d), lambda i: (i, 0)),
        out_shape=jax.ShapeDtypeStruct((m, d), f32),
        compiler_params=_cparams(1, 4 * tm * d * 4 + 4 * MIB),
        name="final_rms",
    )(x, gain.reshape(1, d))


def _ssd_body(z_ref, xp_ref, bc_ref, dtq_ref, dtt_ref, cw_ref, cb_ref, dtb_ref, dtbt_ref, alog_ref, alogt_ref,
              dsk_ref, gn_ref, h0_ref, c0_ref, yg_ref, ssm_ref, ht_ref, xs_ref, xc_ref, *, valid_len):
    q = z_ref.shape[1]
    c = pl.program_id(1)
    n_c = pl.num_programs(1)

    @pl.when(c == 0)
    def _():
        ht_ref[...] = h0_ref[0]
        xs_ref[0:V7X_SUBLANES, :] = c0_ref[0]

    xs_ref[V7X_SUBLANES:V7X_SUBLANES + q, 0:SSD_INNER] = xp_ref[0]
    xs_ref[V7X_SUBLANES:V7X_SUBLANES + q, SSD_INNER:SSD_CONV_DIM] = bc_ref[0]
    lane_step = 512
    for s in range(SSD_CONV_DIM // lane_step):
        sl = slice(s * lane_step, (s + 1) * lane_step)
        acc = cb_ref[:, sl] + cw_ref[SSD_CONV - 1:SSD_CONV, sl] * xs_ref[V7X_SUBLANES:V7X_SUBLANES + q, sl]
        for kk in range(1, SSD_CONV):
            acc = acc + cw_ref[SSD_CONV - 1 - kk:SSD_CONV - kk, sl] * xs_ref[V7X_SUBLANES - kk:V7X_SUBLANES - kk + q, sl]
        xc_ref[:, sl] = jax.nn.silu(acc)
    xs_ref[0:V7X_SUBLANES, :] = xs_ref[q:q + V7X_SUBLANES, :]

    ii = lax.broadcasted_iota(i32, (q, q), 0)
    jj = lax.broadcasted_iota(i32, (q, q), 1)
    causal = ii >= jj
    tril = jnp.where(causal, 1.0, 0.0).astype(bf16)
    triu = jnp.where(jj >= ii, 1.0, 0.0).astype(bf16)

    n_terms = 3
    hrow = lax.broadcasted_iota(i32, (n_terms * SSD_HPG, SSD_GROUP_W), 0) % SSD_HPG
    spread_x = jnp.where(lax.broadcasted_iota(i32, hrow.shape, 1) // SSD_HEAD_DIM == hrow, 1.0, 0.0).astype(bf16)
    prow = lax.broadcasted_iota(i32, (n_terms * SSD_HPG, SSD_HPG * q), 0) % SSD_HPG
    spread_q = jnp.where(lax.broadcasted_iota(i32, prow.shape, 1) // q == prow, 1.0, 0.0).astype(bf16)
    low_half = lax.broadcasted_iota(i32, (q, V7X_LANES), 1) < SSD_HEAD_DIM

    def spread(cols, mat):
        return _dot(jnp.concatenate(_split3(cols), axis=1), mat)

    def group(g):
        dt = jax.nn.softplus(dtq_ref[0, g] + dtb_ref[g])
        dtt = jax.nn.softplus(dtt_ref[0, g] + dtbt_ref[g])
        if valid_len is not None:
            dt = jnp.where(lax.broadcasted_iota(i32, dt.shape, 0) < valid_len, dt, 0.0)
            dtt = jnp.where(lax.broadcasted_iota(i32, dtt.shape, 1) < valid_len, dtt, 0.0)
        adt = dt * (-jnp.exp(alog_ref[g]))
        adtt = dtt * (-jnp.exp(alogt_ref[g]))
        acum = sum(_dot(tril, p) for p in _split3(adt))
        acumt = sum(_dot(p, triu) for p in _split3(adtt))
        last = acum[q - 1:q, :]
        col_q = spread(acum, spread_q)
        e_x = spread(jnp.exp(acum), spread_x)
        w_x = spread(jnp.exp(last - acum) * dt, spread_x)

        xoff = g * SSD_GROUP_W
        boff = SSD_INNER + g * SSD_STATE
        coff = SSD_INNER + SSD_GROUPS * SSD_STATE + g * SSD_STATE
        xg = xc_ref[:, pl.ds(xoff, SSD_GROUP_W)]
        bg = xc_ref[:, pl.ds(boff, SSD_STATE)].astype(bf16)
        cg = xc_ref[:, pl.ds(coff, SSD_STATE)].astype(bf16)
        cbm = _dot_nt(cg, bg)
        htg = ht_ref[:, pl.ds(xoff, SSD_GROUP_W)]
        yoff = _dot(cg, htg.astype(bf16))

        yds = []
        for pr in range(SSD_HPG // 2):
            xp = xg[:, pr * V7X_LANES:(pr + 1) * V7X_LANES]
            halves = (jnp.where(low_half, xp, 0.0).astype(bf16), jnp.where(low_half, 0.0, xp).astype(bf16))
            acc = None
            for r, xh in zip((2 * pr, 2 * pr + 1), halves):
                seg = col_q[:, r * q:(r + 1) * q] - acumt[r:r + 1, :]
                dec = jnp.exp(jnp.where(causal, seg, -jnp.inf))
                part = _dot((cbm * dec * dtt[r:r + 1, :]).astype(bf16), xh)
                acc = part if acc is None else acc + part
            yds.append(acc)
        yd = jnp.concatenate(yds, axis=1)
        yg = yd + yoff * e_x + dsk_ref[:, pl.ds(xoff, SSD_GROUP_W)] * xg

        ht_ref[:, pl.ds(xoff, SSD_GROUP_W)] = htg * e_x[q - 1:q, :] + _dot_tn(bg, (xg * w_x).astype(bf16))

        gz = yg * jax.nn.silu(z_ref[0, :, pl.ds(xoff, SSD_GROUP_W)])
        gzn = gz * lax.rsqrt(jnp.mean(gz * gz, axis=-1, keepdims=True) + NORM_EPS)
        yg_ref[0, :, pl.ds(xoff, SSD_GROUP_W)] = (gzn * gn_ref[:, pl.ds(xoff, SSD_GROUP_W)]).astype(yg_ref.dtype)

    for g in range(SSD_GROUPS):
        group(g)

    @pl.when(c == n_c - 1)
    def _():
        for g in range(SSD_GROUPS):
            sl = slice(g * SSD_GROUP_W, (g + 1) * SSD_GROUP_W)
            ssm_ref[0, sl, :] = ht_ref[:, sl].T


def _ssd_scan(proj, dt_raw, conv0, ssm0, conv_w, conv_b, dt_bias, a_log, d_skip, gate_norm, *, valid_len=None,
              chunk=SSD_CHUNK):
    b_, l_, _ = proj.shape
    q = chunk
    assert l_ % q == 0
    n_c = l_ // q
    g_, r_ = SSD_GROUPS, SSD_HPG
    dtq = dt_raw.reshape(b_, l_, g_, r_).transpose(0, 2, 1, 3)
    dtt = dt_raw.reshape(b_, l_, g_, r_).transpose(0, 2, 3, 1)
    h0t = ssm0.reshape(b_, SSD_INNER, SSD_STATE).transpose(0, 2, 1)
    c0 = jnp.pad(conv0, ((0, 0), (V7X_SUBLANES - (SSD_CONV - 1), 0), (0, 0)))
    dtb = dt_bias.reshape(g_, 1, r_)
    dtbt = dt_bias.reshape(g_, r_, 1)
    alog = a_log.reshape(g_, 1, r_)
    alogt = a_log.reshape(g_, r_, 1)
    dsk = jnp.repeat(d_skip, SSD_HEAD_DIM).reshape(1, SSD_INNER)
    full = lambda shape: pl.BlockSpec(shape, lambda b, c: (0,) * len(shape))
    in_specs = [
        pl.BlockSpec((1, q, SSD_INNER), lambda b, c: (b, c, 0)),
        pl.BlockSpec((1, q, SSD_INNER), lambda b, c: (b, c, 1)),
        pl.BlockSpec((1, q, SSD_BC_W), lambda b, c: (b, c, 2 * SSD_INNER // SSD_BC_W)),
        pl.BlockSpec((1, g_, q, r_), lambda b, c: (b, 0, c, 0)),
        pl.BlockSpec((1, g_, r_, q), lambda b, c: (b, 0, 0, c)),
        full((SSD_CONV, SSD_CONV_DIM)), full((1, SSD_CONV_DIM)),
        full((g_, 1, r_)), full((g_, r_, 1)), full((g_, 1, r_)), full((g_, r_, 1)), full((1, SSD_INNER)),
        full((1, SSD_INNER)),
        pl.BlockSpec((1, SSD_STATE, SSD_INNER), lambda b, c: (b, 0, 0)),
        pl.BlockSpec((1, V7X_SUBLANES, SSD_CONV_DIM), lambda b, c: (b, 0, 0)),
    ]
    out_specs = [
        pl.BlockSpec((1, q, SSD_INNER), lambda b, c: (b, c, 0)),
        pl.BlockSpec((1, SSD_INNER, SSD_STATE), lambda b, c: (b, 0, 0)),
    ]
    yg, ssm = pl.pallas_call(
        functools.partial(_ssd_body, valid_len=valid_len),
        grid=(b_, n_c),
        in_specs=in_specs,
        out_specs=out_specs,
        out_shape=[jax.ShapeDtypeStruct((b_, l_, SSD_INNER), bf16),
                   jax.ShapeDtypeStruct((b_, SSD_INNER, SSD_STATE), f32)],
        scratch_shapes=[pltpu.VMEM((SSD_STATE, SSD_INNER), f32),
                        pltpu.VMEM((q + 2 * V7X_SUBLANES, SSD_CONV_DIM), f32),
                        pltpu.VMEM((q, SSD_CONV_DIM), f32)],
        compiler_params=_cparams(2, 48 * MIB),
        name="ssd_scan",
    )(proj, proj, proj, dtq, dtt, conv_w, conv_b.reshape(1, -1), dtb, dtbt, alog, alogt, dsk,
      gate_norm.reshape(1, -1), h0t, c0)
    return yg, ssm.reshape(b_, SSD_HEADS, SSD_HEAD_DIM, SSD_STATE)


def _trig_body(pos_ref, invf_ref, sgn_ref, cos_ref, sin_ref):
    ang = pos_ref[...].astype(f32) * invf_ref[...]
    rot = lax.broadcasted_iota(i32, ang.shape, 1) < ROPE_DIM
    cos_ref[...] = jnp.where(rot, jnp.cos(ang), 1.0)
    sin_ref[...] = jnp.sin(ang) * sgn_ref[...]


def _trig_tables(pos):
    n = pos.shape[0]
    inv_freq = ROPE_THETA ** (-jnp.arange(ROPE_HALF, dtype=f32) * 2.0 / ROPE_DIM)
    pad = jnp.zeros((MOBA_HEAD_DIM - ROPE_DIM,), f32)
    invf = jnp.concatenate([inv_freq, inv_freq, pad]).reshape(1, MOBA_HEAD_DIM)
    sgn = jnp.concatenate([-jnp.ones((ROPE_HALF,), f32), jnp.ones((ROPE_HALF,), f32), pad]).reshape(1, MOBA_HEAD_DIM)
    out = jax.ShapeDtypeStruct((n, MOBA_HEAD_DIM), f32)
    return pl.pallas_call(_trig_body, out_shape=[out, out], name="rope_tables")(pos.reshape(n, 1).astype(i32), invf, sgn)


def _rotate(x, cos_t, sin_t):
    lane = lax.broadcasted_iota(i32, x.shape, 1)
    partner = jnp.where(lane < ROPE_HALF, pltpu.roll(x, MOBA_HEAD_DIM - ROPE_HALF, 1), pltpu.roll(x, ROPE_HALF, 1))
    return x * cos_t + partner * sin_t


def _prep_body(qkv_ref, cos_ref, sin_ref, q_ref, k_ref, kb_ref, vb_ref, *maybe_kmean, with_kmean):
    cos_t = cos_ref[...]
    sin_t = sin_ref[...]
    for h in range(MOBA_HEADS):
        sl = slice(h * MOBA_HEAD_DIM, (h + 1) * MOBA_HEAD_DIM)
        q_ref[:, sl] = _rotate(qkv_ref[:, sl], cos_t, sin_t)
        kr = _rotate(qkv_ref[:, ATTN_W + h * MOBA_HEAD_DIM:ATTN_W + (h + 1) * MOBA_HEAD_DIM], cos_t, sin_t)
        k_ref[:, sl] = kr
        kb_ref[:, sl] = kr.astype(bf16)
        if with_kmean:
            maybe_kmean[0][0, :, sl] = jnp.sum(kr, axis=0, keepdims=True) * (1.0 / MOBA_BLOCK)
    vb_ref[...] = qkv_ref[:, 2 * ATTN_W:3 * ATTN_W].astype(bf16)


def _moba_prep(qkv, cos_t, sin_t, *, rows, with_kmean):
    m = qkv.shape[0]
    n_tab = cos_t.shape[0] // rows
    row_spec = lambda w: pl.BlockSpec((rows, w), lambda i: (i, 0))
    tab_spec = pl.BlockSpec((rows, MOBA_HEAD_DIM), lambda i: (i % n_tab, 0))
    out_shape = [jax.ShapeDtypeStruct((m, ATTN_W), f32), jax.ShapeDtypeStruct((m, ATTN_W), f32),
                 jax.ShapeDtypeStruct((m, ATTN_W), bf16), jax.ShapeDtypeStruct((m, ATTN_W), bf16)]
    out_specs = [row_spec(ATTN_W)] * 4
    if with_kmean:
        out_shape.append(jax.ShapeDtypeStruct((m // rows, 1, ATTN_W), f32))
        out_specs.append(pl.BlockSpec((1, 1, ATTN_W), lambda i: (i, 0, 0)))
    return pl.pallas_call(
        functools.partial(_prep_body, with_kmean=with_kmean),
        grid=(m // rows,),
        in_specs=[row_spec(3 * ATTN_W), tab_spec, tab_spec],
        out_specs=out_specs,
        out_shape=out_shape,
        compiler_params=_cparams(1, 40 * MIB),
        name="moba_prep",
    )(qkv, cos_t, sin_t)


def _gate_scores(q, km):
    qh, qm, _ = _split3(q)
    kh, kmid, _ = _split3(km)
    return _dot_nt(qh, kh) + _dot_nt(qh, kmid) + _dot_nt(qm, kh)


def _top_blocks(gate, n_take):
    lane = lax.broadcasted_iota(i32, gate.shape, 1)
    picks = []
    for _ in range(n_take):
        mx = jnp.max(gate, axis=-1, keepdims=True)
        idx = jnp.min(jnp.where(gate == mx, lane, gate.shape[1]), axis=-1, keepdims=True)
        picks.append(idx)
        gate = jnp.where(lane == idx, -jnp.inf, gate)
    return picks


ATTN_HEADS_PER_STEP = 4
ATTN_MASKED = -1e30


def _attn_prompt_body(q_ref, kb_ref, vb_ref, km_ref, o_ref, s_ref, qa_ref):
    blk = MOBA_BLOCK
    hd = MOBA_HEAD_DIM
    cur = pl.program_id(2)
    n_blk = km_ref.shape[1]
    n_all = n_blk * blk
    scale = hd ** -0.5
    blk_id = lax.broadcasted_iota(i32, (n_blk, blk), 0)
    past = blk_id < cur
    causal = lax.broadcasted_iota(i32, (blk, blk), 1) <= lax.broadcasted_iota(i32, (blk, blk), 0)
    ones = jnp.ones((blk, hd), bf16)
    lane = lax.broadcasted_iota(i32, (blk, hd), 1)
    eye = jnp.where(lax.broadcasted_iota(i32, (n_blk, hd), 1) == lax.broadcasted_iota(i32, (n_blk, hd), 0),
                    1.0, 0.0).astype(bf16)
    own = pl.multiple_of(cur * blk, blk)
    own_slot = slice(n_all, n_all + blk)

    for hh in range(ATTN_HEADS_PER_STEP):
        hs = slice(hh * hd, (hh + 1) * hd)
        q = q_ref[:, hs]
        gate = jnp.where(past, _gate_scores(km_ref[0, :, hs], q), -jnp.inf)
        sel = jnp.zeros((n_blk, blk), i32)
        for _ in range(min(MOBA_TOPK, n_blk)):
            mx = jnp.max(gate, axis=0, keepdims=True)
            idx = jnp.min(jnp.where(gate == mx, blk_id, n_blk), axis=0, keepdims=True)
            sel = jnp.where((blk_id == idx) & past, 1, sel)
            gate = jnp.where(blk_id == idx, -jnp.inf, gate)
        bias_t = jnp.where(sel > 0, 0.0, ATTN_MASKED).astype(bf16)
        qa_ref[hh, :, 0:hd] = (q * scale).astype(bf16)
        qa_ref[hh, :, hd:2 * hd] = _dot_tn(bias_t, eye).astype(bf16)

    def attend(n_past):
        for hh in range(ATTN_HEADS_PER_STEP):
            hs = slice(hh * hd, (hh + 1) * hd)
            q_aug = qa_ref[hh]
            s = jnp.where(causal, _dot_nt(q_aug[:, 0:hd], kb_ref[pl.ds(own, blk), hs]), -jnp.inf)
            s_ref[hh, :, own_slot] = s
            m2 = jnp.maximum(s[:, :hd], s[:, hd:])
            for n in range(n_past):
                keys = slice(n * blk, (n + 1) * blk)
                k_aug = jnp.concatenate([kb_ref[keys, hs], jnp.where(lane == n, 1.0, 0.0).astype(bf16)], axis=1)
                s = _dot_nt(q_aug, k_aug)
                s_ref[hh, :, keys] = s
                m2 = jnp.maximum(m2, jnp.maximum(s[:, :hd], s[:, hd:]))
            m = jnp.broadcast_to(jnp.max(m2, axis=-1, keepdims=True), (blk, blk))
            p = jnp.exp(s_ref[hh, :, own_slot] - m).astype(bf16)
            acc = _dot(p, jnp.concatenate([vb_ref[pl.ds(own, blk), hs], ones], axis=1))
            for n in range(n_past):
                keys = slice(n * blk, (n + 1) * blk)
                p = jnp.exp(s_ref[hh, :, keys] - m).astype(bf16)
                acc = acc + _dot(p, jnp.concatenate([vb_ref[keys, hs], ones], axis=1))
            o_ref[:, hs] = (acc[:, :hd] / acc[:, hd:]).astype(o_ref.dtype)

    for pair in range((n_blk + 1) // 2):
        pl.when(cur // 2 == pair)(functools.partial(attend, min(2 * pair + 1, n_blk - 1)))


def _moba_attn_prompt(q, kb, vb, kmean, b_, l_):
    n_blk = l_ // MOBA_BLOCK
    w = ATTN_HEADS_PER_STEP * MOBA_HEAD_DIM
    return pl.pallas_call(
        _attn_prompt_body,
        grid=(b_, MOBA_HEADS // ATTN_HEADS_PER_STEP, n_blk),
        in_specs=[
            pl.BlockSpec((MOBA_BLOCK, w), lambda b, h, t: (b * n_blk + t, h)),
            pl.BlockSpec((l_, w), lambda b, h, t: (b, h)),
            pl.BlockSpec((l_, w), lambda b, h, t: (b, h)),
            pl.BlockSpec((1, n_blk, w), lambda b, h, t: (b, 0, h)),
        ],
        out_specs=pl.BlockSpec((MOBA_BLOCK, w), lambda b, h, t: (b * n_blk + t, h)),
        out_shape=jax.ShapeDtypeStruct((b_ * l_, ATTN_W), bf16),
        scratch_shapes=[pltpu.VMEM((ATTN_HEADS_PER_STEP, MOBA_BLOCK, l_ + MOBA_BLOCK), f32),
                        pltpu.VMEM((ATTN_HEADS_PER_STEP, MOBA_BLOCK, 2 * MOBA_HEAD_DIM), bf16)],
        compiler_params=_cparams(3, 32 * MIB),
        name="moba_attn_prompt",
    )(q, kb, vb, kmean)


PAGES_PER_BLOCK = MOBA_BLOCK // PAGE_SIZE
KMEAN_BLOCKS_PER_STEP = 8
KMEAN_PAGES_PER_STEP = KMEAN_BLOCKS_PER_STEP * PAGES_PER_BLOCK


def _kmean_body(pt_ref, *refs):
    page_refs, o_ref = refs[:-1], refs[-1]
    for blk in range(KMEAN_BLOCKS_PER_STEP):
        tot = None
        for p in range(PAGES_PER_BLOCK):
            part = jnp.sum(page_refs[blk * PAGES_PER_BLOCK + p][...], axis=0)
            tot = part if tot is None else tot + part
        tot = tot * (1.0 / MOBA_BLOCK)
        for h in range(MOBA_HEADS):
            o_ref[0, blk:blk + 1, h * MOBA_HEAD_DIM:(h + 1) * MOBA_HEAD_DIM] = tot[h:h + 1, :]


def _cache_kmean(cache, layer, page_table):
    b_, n_pages = page_table.shape
    n_steps = n_pages // KMEAN_PAGES_PER_STEP
    pt_flat = page_table.reshape(-1)

    def page_spec(p):
        return pl.BlockSpec((None, None, PAGE_SIZE, MOBA_HEADS, MOBA_HEAD_DIM),
                            lambda b, s, pt: (layer, pt[b * n_pages + s * KMEAN_PAGES_PER_STEP + p], 0, 0, 0))

    return pl.pallas_call(
        _kmean_body,
        grid_spec=pltpu.PrefetchScalarGridSpec(
            num_scalar_prefetch=1,
            grid=(b_, n_steps),
            in_specs=[page_spec(p) for p in range(KMEAN_PAGES_PER_STEP)],
            out_specs=pl.BlockSpec((1, KMEAN_BLOCKS_PER_STEP, ATTN_W), lambda b, s, pt: (b, s, 0)),
        ),
        out_shape=jax.ShapeDtypeStruct((b_, n_pages // PAGES_PER_BLOCK, ATTN_W), f32),
        compiler_params=_cparams(2, 40 * MIB),
        name="cache_kmean",
    )(pt_flat, *([cache] * KMEAN_PAGES_PER_STEP))


def _sample_topk_body(q_ref, km_ref, o_ref, g_ref):
    rows = q_ref.shape[1]
    for h in range(MOBA_HEADS):
        sl = slice(h * MOBA_HEAD_DIM, (h + 1) * MOBA_HEAD_DIM)
        g_ref[h * rows:(h + 1) * rows, :] = _gate_scores(q_ref[0, :, sl], km_ref[0, :, sl])
    picks = _top_blocks(g_ref[...], MOBA_TOPK)
    lane = lax.broadcasted_iota(i32, o_ref.shape[1:], 1)
    out = jnp.zeros(o_ref.shape[1:], i32)
    for s, idx in enumerate(picks):
        out = jnp.where(lane == s, idx, out)
    o_ref[0] = out


def _sample_topk(q8, kmean):
    b_, rows, _ = q8.shape
    n_blk = kmean.shape[1]
    return pl.pallas_call(
        _sample_topk_body,
        grid=(b_,),
        in_specs=[pl.BlockSpec((1, rows, ATTN_W), lambda b: (b, 0, 0)),
                  pl.BlockSpec((1, n_blk, ATTN_W), lambda b: (b, 0, 0))],
        out_specs=pl.BlockSpec((1, MOBA_HEADS * rows, V7X_LANES), lambda b: (b, 0, 0)),
        out_shape=jax.ShapeDtypeStruct((b_, MOBA_HEADS * rows, V7X_LANES), i32),
        scratch_shapes=[pltpu.VMEM((MOBA_HEADS * rows, n_blk), f32)],
        compiler_params=_cparams(1, 16 * MIB),
        name="sample_topk",
    )(q8, kmean)


def _attn_sample_body(sel_ref, pt_ref, q_ref, kn_ref, vn_ref, ck_hbm, cv_hbm, o_ref, kbuf, vbuf, sems,
                      *, layer, n_tok, n_pages):
    pages_per_tok = MOBA_TOPK * PAGES_PER_BLOCK
    keys_per_tok = pages_per_tok * PAGE_SIZE
    n_heads = pl.num_programs(1)
    step = pl.program_id(0) * n_heads + pl.program_id(1)
    n_steps = pl.num_programs(0) * n_heads
    slot = lax.rem(step, 2)

    def page_copies(step_, slot_, known_source):
        b_ = step_ // n_heads
        h_ = lax.rem(step_, n_heads) if known_source else 0
        out = []
        for t in range(n_tok):
            for sp in range(pages_per_tok):
                page = 0
                if known_source:
                    blk = sel_ref[step_ * (n_tok * MOBA_TOPK) + t * MOBA_TOPK + sp // PAGES_PER_BLOCK]
                    page = pt_ref[b_ * n_pages + blk * PAGES_PER_BLOCK + sp % PAGES_PER_BLOCK]
                dst = (slot_, t, pl.ds(sp * PAGE_SIZE, PAGE_SIZE))
                out.append(pltpu.make_async_copy(ck_hbm.at[layer, page, :, h_, :], kbuf.at[dst], sems.at[0, slot_]))
                out.append(pltpu.make_async_copy(cv_hbm.at[layer, page, :, h_, :], vbuf.at[dst], sems.at[1, slot_]))
        return out

    def start_all(copies):
        for n, cp in enumerate(copies):
            cp.start(priority=n % 2)

    @pl.when(step == 0)
    def _():
        start_all(page_copies(step, slot, True))

    @pl.when(step + 1 < n_steps)
    def _():
        start_all(page_copies(step + 1, 1 - slot, True))

    for cp in page_copies(step, slot, False):
        cp.wait()

    rows = q_ref.shape[1]
    scale = MOBA_HEAD_DIM ** -0.5
    qb = q_ref[0].astype(bf16)
    row = lax.broadcasted_iota(i32, (rows, keys_per_tok), 0)
    k_refs = [kbuf.at[slot, t] for t in range(n_tok)]
    v_refs = [vbuf.at[slot, t] for t in range(n_tok)]
    scores = []
    for t in range(n_tok):
        s = _dot_nt(qb, k_refs[t][...].astype(bf16)) * scale
        scores.append(jnp.where(row == t, s, -jnp.inf))
    r8 = lax.broadcasted_iota(i32, (rows, rows), 0)
    c8 = lax.broadcasted_iota(i32, (rows, rows), 1)
    s_own = _dot_nt(qb, kn_ref[0]) * scale
    s_own = jnp.where((c8 <= r8) & (c8 < n_tok), s_own, -jnp.inf)
    m = jnp.max(s_own, axis=-1, keepdims=True)
    for s in scores:
        m = jnp.maximum(m, jnp.max(s, axis=-1, keepdims=True))
    m = jnp.where(m == -jnp.inf, 0.0, m)
    p_own = jnp.exp(s_own - m)
    l = jnp.sum(p_own, axis=-1, keepdims=True)
    acc = _dot(p_own.astype(bf16), vn_ref[0])
    for t in range(n_tok):
        p = jnp.exp(scores[t] - m)
        l = l + jnp.sum(p, axis=-1, keepdims=True)
        acc = acc + _dot(p.astype(bf16), v_refs[t][...].astype(bf16))
    l = jnp.where(l == 0.0, 1.0, l)
    o_ref[0] = (acc / l).astype(o_ref.dtype)


def _moba_attn_sample(sel, page_table, q8, kn8, vn8, cache_k, cache_v, layer, n_tok):
    b_, rows, _ = q8.shape
    n_pages = page_table.shape[1]
    sel_flat = sel.reshape(-1).astype(i32)
    pt_flat = page_table.reshape(-1).astype(i32)
    head_spec = pl.BlockSpec((1, rows, MOBA_HEAD_DIM), lambda b, h, sel_r, pt_r: (b, 0, h))
    page_buf = pltpu.VMEM((2, n_tok, MOBA_TOPK * MOBA_BLOCK, MOBA_HEAD_DIM), f32)
    return pl.pallas_call(
        functools.partial(_attn_sample_body, layer=layer, n_tok=n_tok, n_pages=n_pages),
        grid_spec=pltpu.PrefetchScalarGridSpec(
            num_scalar_prefetch=2,
            grid=(b_, MOBA_HEADS),
            in_specs=[head_spec, head_spec, head_spec,
                      pl.BlockSpec(memory_space=pl.ANY), pl.BlockSpec(memory_space=pl.ANY)],
            out_specs=head_spec,
            scratch_shapes=[page_buf, page_buf, pltpu.SemaphoreType.DMA((2, 2))],
        ),
        out_shape=jax.ShapeDtypeStruct((b_, rows, ATTN_W), bf16),
        compiler_params=_cparams(2, 24 * MIB),
        name="moba_attn_sample",
    )(sel_flat, pt_flat, q8, kn8, vn8, cache_k, cache_v)


def _gmlp_body(uv_ref, lg_ref, lb_ref, ws_ref, bst_ref, o_ref, *maybe_v, with_v):
    n = GMLP_CHUNK
    n_chunks = uv_ref.shape[0] // n
    vf = uv_ref[:, GMLP_DIM:2 * GMLP_DIM]
    mu = jnp.mean(vf, axis=-1, keepdims=True)
    var = jnp.mean(jnp.square(vf - mu), axis=-1, keepdims=True)
    v = (vf - mu) * lax.rsqrt(var + NORM_EPS) * lg_ref[...] + lb_ref[...]
    if with_v:
        maybe_v[0][...] = v
    ii = lax.broadcasted_iota(i32, (n, n), 0)
    jj = lax.broadcasted_iota(i32, (n, n), 1)
    for g in range(GMLP_GROUPS):
        sl = slice(g * GMLP_GROUP_DIM, (g + 1) * GMLP_GROUP_DIM)
        w = jnp.where(jj <= ii, ws_ref[g], 0.0).astype(bf16)
        panels = jnp.concatenate([v[c * n:(c + 1) * n, sl] for c in range(n_chunks)], axis=1).astype(bf16)
        s = _dot(w, panels) + bst_ref[:, g:g + 1]
        for c in range(n_chunks):
            rows = slice(c * n, (c + 1) * n)
            o_ref[rows, sl] = (uv_ref[rows, sl] * s[:, c * GMLP_GROUP_DIM:(c + 1) * GMLP_GROUP_DIM]).astype(o_ref.dtype)


def _gmlp_gate(uv, ln_g, ln_b, w_s, b_s, *, with_v):
    m = uv.shape[0]
    n = GMLP_CHUNK
    rows = n * next(c for c in (GMLP_CHUNKS_PER_STEP, 2, 1) if (m // n) % c == 0)
    out_shape = [jax.ShapeDtypeStruct((m, GMLP_DIM), bf16)]
    out_specs = [pl.BlockSpec((rows, GMLP_DIM), lambda i: (i, 0))]
    if with_v:
        out_shape.append(jax.ShapeDtypeStruct((m, GMLP_DIM), f32))
        out_specs.append(pl.BlockSpec((rows, GMLP_DIM), lambda i: (i, 0)))
    return pl.pallas_call(
        functools.partial(_gmlp_body, with_v=with_v),
        grid=(m // rows,),
        in_specs=[pl.BlockSpec((rows, 2 * GMLP_DIM), lambda i: (i, 0)),
                  pl.BlockSpec((1, GMLP_DIM), lambda i: (0, 0)),
                  pl.BlockSpec((1, GMLP_DIM), lambda i: (0, 0)),
                  pl.BlockSpec((GMLP_GROUPS, n, n), lambda i: (0, 0, 0)),
                  pl.BlockSpec((n, GMLP_GROUPS), lambda i: (0, 0))],
        out_specs=out_specs,
        out_shape=out_shape,
        compiler_params=_cparams(1, 40 * MIB),
        name="gmlp_gate",
    )(uv, ln_g.reshape(1, -1), ln_b.reshape(1, -1), w_s, b_s.T)


def _pad_rows(x, rows):
    return jnp.pad(x, ((0, 0), (0, rows - x.shape[1]), (0, 0)))


def _ssd_group(proj, dt_raw, b_, l_, conv0, ssm0, scan_w):
    proj3 = proj.reshape(b_, l_, SSD_MAIN_W)
    dt3 = dt_raw.reshape(b_, l_, SSD_HEADS)
    if l_ < SSD_CONV - 1:
        new_conv = jnp.concatenate([conv0, proj3[:, :, SSD_INNER:]], axis=1)[:, -(SSD_CONV - 1):]
    else:
        new_conv = proj3[:, l_ - (SSD_CONV - 1):, SSD_INNER:]
    chunk = SSD_CHUNK if l_ >= SSD_CHUNK else -(-l_ // SSD_SHORT_CHUNK_ROWS) * SSD_SHORT_CHUNK_ROWS
    lp = -(-l_ // chunk) * chunk
    valid_len = None
    if lp != l_:
        assert lp == chunk
        proj3 = _pad_rows(proj3, lp)
        dt3 = _pad_rows(dt3, lp)
        valid_len = l_
    yg, ssm = _ssd_scan(proj3, dt3, conv0, ssm0, *scan_w, valid_len=valid_len, chunk=chunk)
    if lp != l_:
        yg = yg[:, :l_]
    return yg.reshape(b_ * l_, SSD_INNER), new_conv, ssm


def _ssd_layer(hp, hs, bp, lp_, bs, ls, gain, conv0_p, ssm0_p, conv0_s, ssm0_s, j, w_in_t, scan_w, w_out):
    proj_p, proj_s, dt_p, dt_s = _mm(hp, [w_in_t], layer=j, w_t=True, n_out=SSD_MAIN_W, tail=SSD_HEADS, gain=gain,
                                     side_x=hs)
    yg_p, conv_p, ssm_p = _ssd_group(proj_p, dt_p, bp, lp_, conv0_p, ssm0_p, scan_w)
    yg_s, conv_s, ssm_s = _ssd_group(proj_s, dt_s, bs, ls, conv0_s, ssm0_s, scan_w)
    hp, hs = _mm(yg_p, [w_out], layer=j, res=hp, side_x=yg_s, side_res=hs)
    return hp, hs, (conv_p, ssm_p), (conv_s, ssm_s)


def _ffn_layer(hp, hs, i, gain, w_gate, w_up, w_down):
    tp, ts = _mm(hp, [w_gate, w_up], layer=i, gain=gain, epi="swiglu", out_dtype=bf16, side_x=hs, max_rows=1024)
    return _mm(tp, [w_down], layer=i, res=hp, side_x=ts, side_res=hs)


def kernel(x_prompt, x_sample, state_ssm, state_conv, cache_k, cache_v, page_table, norm_mix, norm_ffn, norm_final, ssd_w_in, ssd_conv_w, ssd_conv_b, ssd_dt_bias, ssd_a_log, ssd_d, ssd_gate_norm, ssd_w_out, moba_w_qkv, moba_w_o, gmlp_w_in, gmlp_ln_g, gmlp_ln_b, gmlp_w_s, gmlp_b_s, gmlp_w_out, ffn_w_gate, ffn_w_up, ffn_w_down):
    bp, lp_, d = x_prompt.shape
    bs, ls, _ = x_sample.shape
    hp = x_prompt.reshape(bp * lp_, d)
    hs = x_sample.reshape(bs * ls, d)
    n_pages = page_table.shape[1]
    past = n_pages * PAGE_SIZE
    tok_rows = V7X_SUBLANES
    assert ls <= tok_rows and past % MOBA_BLOCK == 0 and lp_ % MOBA_BLOCK == 0

    ssm_p, ssm_s, conv_p, conv_s = [], [], [], []
    k_p, v_p, k_s, v_s, gv_s = [], [], [], [], []
    for i in range(DEPTH):
        kind = i % N_MIXERS
        j = i // N_MIXERS
        if kind == 0:
            w_in_t = jnp.swapaxes(ssd_w_in, 1, 2)
            scan_w = (ssd_conv_w[j], ssd_conv_b[j], ssd_dt_bias[j], ssd_a_log[j], ssd_d[j], ssd_gate_norm[j])
            zero_conv = jnp.zeros((bp, SSD_CONV - 1, SSD_CONV_DIM), f32)
            zero_ssm = jnp.zeros((bp, SSD_HEADS, SSD_HEAD_DIM, SSD_STATE), f32)
            hp, hs, (cp_, sp_), (cs_, ss_) = _ssd_layer(hp, hs, bp, lp_, bs, ls, norm_mix[i], zero_conv, zero_ssm,
                                                        state_conv[j], state_ssm[j], j, w_in_t, scan_w, ssd_w_out)
            conv_p.append(cp_)
            ssm_p.append(sp_)
            conv_s.append(cs_)
            ssm_s.append(ss_)
        elif kind == 1:
            qkv, qkv_s = _mm(hp, [moba_w_qkv], layer=j, gain=norm_mix[i], side_x=hs)
            cos_t, sin_t = _trig_tables(jnp.arange(lp_, dtype=i32))
            q, k, kb, vb, kmean = _moba_prep(qkv, cos_t, sin_t, rows=MOBA_BLOCK, with_kmean=True)
            o = _moba_attn_prompt(q, kb, vb, kmean.reshape(bp, lp_ // MOBA_BLOCK, ATTN_W), bp, lp_)
            k_p.append(k.reshape(bp, lp_, MOBA_HEADS, MOBA_HEAD_DIM))
            v_p.append(qkv[:, 2 * ATTN_W:].reshape(bp, lp_, MOBA_HEADS, MOBA_HEAD_DIM))
            pos_s = past + (jnp.arange(bs * ls, dtype=i32) % ls)
            cos_s, sin_s = _trig_tables(pos_s)
            q_s, kn, knb, vnb = _moba_prep(qkv_s, cos_s, sin_s, rows=bs * ls, with_kmean=False)
            kmean_c = _cache_kmean(cache_k, j, page_table)
            q8 = _pad_rows(q_s.reshape(bs, ls, ATTN_W), tok_rows)
            kn8 = _pad_rows(knb.reshape(bs, ls, ATTN_W), tok_rows)
            vn8 = _pad_rows(vnb.reshape(bs, ls, ATTN_W), tok_rows)
            top = _sample_topk(q8, kmean_c)
            sel = top.reshape(bs, MOBA_HEADS, tok_rows, V7X_LANES)[:, :, :ls, :MOBA_TOPK]
            o_s = _moba_attn_sample(sel, page_table, q8, kn8, vn8, cache_k, cache_v, j, ls)
            k_s.append(kn.reshape(bs, ls, MOBA_HEADS, MOBA_HEAD_DIM))
            v_s.append(qkv_s[:, 2 * ATTN_W:].reshape(bs, ls, MOBA_HEADS, MOBA_HEAD_DIM))
            hp, hs = _mm(o, [moba_w_o], layer=j, res=hp, side_x=o_s[:, :ls].reshape(bs * ls, ATTN_W), side_res=hs)
        else:
            w = (gmlp_ln_g[j], gmlp_ln_b[j], gmlp_w_s[j], gmlp_b_s[j])
            uv, uv_s = _mm(hp, [gmlp_w_in], layer=j, gain=norm_mix[i], epi="gelu", side_x=hs)
            (gated,) = _gmlp_gate(uv, *w, with_v=False)
            uv_s = _pad_rows(uv_s.reshape(bs, ls, 2 * GMLP_DIM), GMLP_CHUNK).reshape(bs * GMLP_CHUNK, 2 * GMLP_DIM)
            gated_s, v_rows = _gmlp_gate(uv_s, *w, with_v=True)
            gated_s = gated_s.reshape(bs, GMLP_CHUNK, GMLP_DIM)[:, :ls].reshape(bs * ls, GMLP_DIM)
            gv_s.append(v_rows.reshape(bs, GMLP_CHUNK, GMLP_DIM)[:, :ls])
            hp, hs = _mm(gated, [gmlp_w_out], layer=j, res=hp, side_x=gated_s, side_res=hs)
        hp, hs = _ffn_layer(hp, hs, i, norm_ffn[i], ffn_w_gate, ffn_w_up, ffn_w_down)
    y_prompt = _rms_norm(hp, norm_final).reshape(bp, lp_, d)
    y_sample = _rms_norm(hs, norm_final).reshape(bs, ls, d)
    return (y_prompt, y_sample, jnp.stack(ssm_p), jnp.stack(ssm_s), jnp.stack(conv_p), jnp.stack(conv_s),
            jnp.stack(k_p), jnp.stack(v_p), jnp.stack(k_s), jnp.stack(v_s), jnp.stack(gv_s))
```
